```python
import jax, jax.numpy as jnp
from jax import lax
import numpy as np

D_MODEL = 1024
BATCH = 8
SEQ = 2048
DEPTH = 1
DEC_BATCH = 16
DEC_SEQ = 2048
PAST_LEN = 128

GRID_W = 64
EPS = 1e-6
ATT_HEADS = 8
ATT_KV_HEADS = 2
ATT_GROUP = ATT_HEADS // ATT_KV_HEADS
ATT_HEAD_DIM = 64
ROPE_AXIS_DIM = ATT_HEAD_DIM // 2
ROPE_THETA = 10000.0
Q_BLOCK = 128
M_HEADS = 4
M_HEAD_DIM = 128
M_CHUNK = 64
FORGET_BIAS_LO = 3.0
FORGET_BIAS_HI = 6.0
D_FF = 2816
ATT_WIDTH = ATT_HEADS * ATT_HEAD_DIM
KV_WIDTH = ATT_KV_HEADS * ATT_HEAD_DIM
M_WIDTH = M_HEADS * M_HEAD_DIM
N_GATE_COLS = 4 * M_HEADS
IN_SPLITS = (ATT_WIDTH, KV_WIDTH, KV_WIDTH, M_WIDTH, M_WIDTH, M_WIDTH, N_GATE_COLS, M_WIDTH, D_MODEL, D_MODEL)
D_IN = sum(IN_SPLITS)

kernel_name = 'hybrid_gqa_mlstm_macaron_encoder'


def rms_norm(x, g):
    xf = x.astype(jnp.float32)
    y = xf * lax.rsqrt(jnp.mean(xf * xf, axis=-1, keepdims=True) + EPS)
    return (y * g.astype(jnp.float32)).astype(x.dtype)


def swiglu(x, w_gate, w_up, w_down):
    return (jax.nn.silu(x @ w_gate) * (x @ w_up)) @ w_down


def axial_rope_angles(seq_len):
    rows = seq_len // GRID_W
    row = jnp.repeat(jnp.arange(rows, dtype=jnp.float32), GRID_W)
    col = jnp.tile(jnp.arange(GRID_W, dtype=jnp.float32), rows)
    inv_freq = ROPE_THETA ** (-jnp.arange(0, ROPE_AXIS_DIM, 2, dtype=jnp.float32) / ROPE_AXIS_DIM)
    ang = jnp.concatenate([row[:, None] * inv_freq, col[:, None] * inv_freq], axis=-1)
    return jnp.cos(ang), jnp.sin(ang)


def apply_rope(x, cos, sin):
    xf = x.astype(jnp.float32).reshape(x.shape[:-1] + (ATT_HEAD_DIM // 2, 2))
    x0, x1 = xf[..., 0], xf[..., 1]
    c, s = cos[None, :, None, :], sin[None, :, None, :]
    out = jnp.stack([x0 * c - x1 * s, x0 * s + x1 * c], axis=-1)
    return out.reshape(x.shape).astype(x.dtype)


def gqa_block_attention(q, k, v):
    B, S = q.shape[:2]
    nb = S // Q_BLOCK
    qb = q.reshape(B, nb, Q_BLOCK, ATT_KV_HEADS, ATT_GROUP, ATT_HEAD_DIM).transpose(1, 0, 2, 3, 4, 5)
    scale = ATT_HEAD_DIM ** -0.5

    def one_block(qi):
        s = jnp.einsum('bqkgd,bskd->bkgqs', qi, k, preferred_element_type=jnp.float32) * scale
        p = jax.nn.softmax(s, axis=-1).astype(v.dtype)
        return jnp.einsum('bkgqs,bskd->bqkgd', p, v)

    o = lax.map(one_block, qb)
    return o.transpose(1, 0, 2, 3, 4, 5).reshape(B, S, ATT_WIDTH)


def mlstm_direction(q, k, v, logi, logf):
    B, H, S, d = q.shape
    nc = S // M_CHUNK

    def chunks(a):
        a = a.reshape((B, H, nc, M_CHUNK) + a.shape[3:])
        return jnp.moveaxis(a, 2, 0)

    tril = jnp.tril(jnp.ones((M_CHUNK, M_CHUNK), dtype=bool))

    def step(carry, inp):
        C, n, m = carry
        qc, kc, vc, li, lf = inp
        b = jnp.cumsum(lf, axis=-1)
        D = b[..., :, None] - b[..., None, :] + li[..., None, :]
        D = jnp.where(tril, D, -jnp.inf)
        inter = b + m[..., None]
        m_t = jnp.maximum(inter, jnp.max(D, axis=-1))
        w_inter = jnp.exp(inter - m_t)
        S_qk = jnp.einsum('bhtd,bhsd->bhts', qc, kc) * jnp.exp(D - m_t[..., None])
        num = jnp.einsum('bhts,bhse->bhte', S_qk, vc) + w_inter[..., None] * jnp.einsum('bhtd,bhde->bhte', qc, C)
        den = jnp.sum(S_qk, axis=-1) + w_inter * jnp.einsum('bhtd,bhd->bht', qc, n)
        h = num / jnp.maximum(jnp.abs(den), jnp.exp(-m_t))[..., None]
        bL = b[..., -1]
        g = bL[..., None] - b + li
        m_new = jnp.maximum(bL + m, jnp.max(g, axis=-1))
        w_old = jnp.exp(bL + m - m_new)
        w_s = jnp.exp(g - m_new[..., None])
        C_new = w_old[..., None, None] * C + jnp.einsum('bhs,bhsd,bhse->bhde', w_s, kc, vc)
        n_new = w_old[..., None] * n + jnp.einsum('bhs,bhsd->bhd', w_s, kc)
        return (C_new, n_new, m_new), h

    init = (jnp.zeros((B, H, d, d), jnp.float32), jnp.zeros((B, H, d), jnp.float32), jnp.zeros((B, H), jnp.float32))
    _, h = lax.scan(step, init, (chunks(q), chunks(k), chunks(v), chunks(logi), chunks(logf)))
    return jnp.moveaxis(h, 0, 2).reshape(B, H, S, d)


def mlstm_bidirectional(q, k, v, gate_pre, gate_bias, o_pre, head_norm):
    B, S, _ = q.shape

    def to_heads(a):
        return a.astype(jnp.float32).reshape(B, S, M_HEADS, M_HEAD_DIM).transpose(0, 2, 1, 3)

    qh, kh, vh = to_heads(q), to_heads(k) * (M_HEAD_DIM ** -0.5), to_heads(v)
    gp = (gate_pre.astype(jnp.float32) + gate_bias.astype(jnp.float32)).reshape(B, S, 4, M_HEADS).transpose(2, 0, 3, 1)
    i_fwd, f_fwd = gp[0], jax.nn.log_sigmoid(gp[1])
    i_bwd, f_bwd = gp[2], jax.nn.log_sigmoid(gp[3])
    h_fwd = mlstm_direction(qh, kh, vh, i_fwd, f_fwd)

    def flip(a):
        return jnp.flip(a, axis=2)

    h_bwd = flip(mlstm_direction(flip(qh), flip(kh), flip(vh), flip(i_bwd), flip(f_bwd)))
    h = (h_fwd + h_bwd).transpose(0, 2, 1, 3)
    h = h * lax.rsqrt(jnp.mean(h * h, axis=-1, keepdims=True) + EPS)
    h = h.reshape(B, S, M_WIDTH) * head_norm.astype(jnp.float32)
    return (jax.nn.sigmoid(o_pre.astype(jnp.float32)) * h).astype(q.dtype)


def encoder_layer(x, cos, sin, ffn1_norm, ffn1_w_gate, ffn1_w_up, ffn1_w_down, mix_norm, w_in, q_norm, k_norm,
                  mlstm_gate_bias, mlstm_head_norm, w_up_att, w_up_mlstm, w_out,
                  ffn2_norm, ffn2_w_gate, ffn2_w_up, ffn2_w_down):
    B, S, _ = x.shape
    x = x + 0.5 * swiglu(rms_norm(x, ffn1_norm), ffn1_w_gate, ffn1_w_up, ffn1_w_down)
    h = rms_norm(x, mix_norm)
    proj = h @ w_in
    split_at = [int(i) for i in np.cumsum(IN_SPLITS)[:-1]]
    aq, ak, av, mq, mk, mv, gate_pre, o_pre, g_att, g_mlstm = jnp.split(proj, split_at, axis=-1)
    aq = apply_rope(rms_norm(aq.reshape(B, S, ATT_HEADS, ATT_HEAD_DIM), q_norm), cos, sin)
    ak = apply_rope(rms_norm(ak.reshape(B, S, ATT_KV_HEADS, ATT_HEAD_DIM), k_norm), cos, sin)
    av = av.reshape(B, S, ATT_KV_HEADS, ATT_HEAD_DIM)
    att = gqa_block_attention(aq, ak, av)
    mem = mlstm_bidirectional(mq, mk, mv, gate_pre, mlstm_gate_bias, o_pre, mlstm_head_norm)
    merged = jax.nn.sigmoid(g_att) * (att @ w_up_att) + jax.nn.sigmoid(g_mlstm) * (mem @ w_up_mlstm)
    x = x + merged @ w_out
    x = x + 0.5 * swiglu(rms_norm(x, ffn2_norm), ffn2_w_gate, ffn2_w_up, ffn2_w_down)
    return x


def trunk(x, ffn1_norm, ffn1_w_gate, ffn1_w_up, ffn1_w_down, mix_norm, w_in, q_norm, k_norm,
          mlstm_gate_bias, mlstm_head_norm, w_up_att, w_up_mlstm, w_out,
          ffn2_norm, ffn2_w_gate, ffn2_w_up, ffn2_w_down, final_norm):
    cos, sin = axial_rope_angles(x.shape[1])
    for l in range(DEPTH):
        x = encoder_layer(x, cos, sin, ffn1_norm[l], ffn1_w_gate[l], ffn1_w_up[l], ffn1_w_down[l], mix_norm[l],
                          w_in[l], q_norm[l], k_norm[l], mlstm_gate_bias[l], mlstm_head_norm[l], w_up_att[l],
                          w_up_mlstm[l], w_out[l], ffn2_norm[l], ffn2_w_gate[l], ffn2_w_up[l], ffn2_w_down[l])
    return rms_norm(x, final_norm)


def setup_inputs(seed: int = 0) -> dict:
    key = jax.random.key(seed)
    ks = jax.random.split(key, 24)
    L = DEPTH

    def dense(k, fan_in, shape):
        return jax.random.normal(k, shape, jnp.float32) * (fan_in ** -0.5)

    def gain(k, shape):
        return 1.0 + 0.02 * jax.random.normal(k, shape, jnp.float32)

    b_i = 0.1 * jax.random.normal(ks[20], (L, 2, M_HEADS), jnp.float32)
    b_f = jnp.linspace(FORGET_BIAS_LO, FORGET_BIAS_HI, M_HEADS, dtype=jnp.float32)[None, None, :] \
        + 0.1 * jax.random.normal(ks[21], (L, 2, M_HEADS), jnp.float32)
    gate_bias = jnp.stack([b_i[:, 0], b_f[:, 0], b_i[:, 1], b_f[:, 1]], axis=1).reshape(L, N_GATE_COLS)
    return {
        'x_prompt': jax.random.normal(ks[0], (BATCH, SEQ, D_MODEL), jnp.float32),
        'x_sample': jax.random.normal(ks[1], (DEC_BATCH, DEC_SEQ, D_MODEL), jnp.float32),
        'ffn1_norm': gain(ks[2], (L, D_MODEL)),
        'ffn1_w_gate': dense(ks[3], D_MODEL, (L, D_MODEL, D_FF)),
        'ffn1_w_up': dense(ks[4], D_MODEL, (L, D_MODEL, D_FF)),
        'ffn1_w_down': dense(ks[5], D_FF, (L, D_FF, D_MODEL)),
        'mix_norm': gain(ks[6], (L, D_MODEL)),
        'w_in': dense(ks[7], D_MODEL, (L, D_MODEL, D_IN)),
        'q_norm': gain(ks[8], (L, ATT_HEAD_DIM)),
        'k_norm': gain(ks[9], (L, ATT_HEAD_DIM)),
        'mlstm_gate_bias': gate_bias,
        'mlstm_head_norm': gain(ks[10], (L, M_WIDTH)),
        'w_up_att': dense(ks[11], ATT_WIDTH, (L, ATT_WIDTH, D_MODEL)),
        'w_up_mlstm': dense(ks[12], M_WIDTH, (L, M_WIDTH, D_MODEL)),
        'w_out': dense(ks[13], D_MODEL, (L, D_MODEL, D_MODEL)),
        'ffn2_norm': gain(ks[14], (L, D_MODEL)),
        'ffn2_w_gate': dense(ks[15], D_MODEL, (L, D_MODEL, D_FF)),
        'ffn2_w_up': dense(ks[16], D_MODEL, (L, D_MODEL, D_FF)),
        'ffn2_w_down': dense(ks[17], D_FF, (L, D_FF, D_MODEL)),
        'final_norm': gain(ks[18], (D_MODEL,)),
    }


def reference(x_prompt, x_sample, ffn1_norm, ffn1_w_gate, ffn1_w_up, ffn1_w_down, mix_norm, w_in, q_norm, k_norm,
              mlstm_gate_bias, mlstm_head_norm, w_up_att, w_up_mlstm, w_out,
              ffn2_norm, ffn2_w_gate, ffn2_w_up, ffn2_w_down, final_norm):
    y_prompt = trunk(x_prompt, ffn1_norm, ffn1_w_gate, ffn1_w_up, ffn1_w_down, mix_norm, w_in, q_norm, k_norm,
                     mlstm_gate_bias, mlstm_head_norm, w_up_att, w_up_mlstm, w_out,
                     ffn2_norm, ffn2_w_gate, ffn2_w_up, ffn2_w_down, final_norm)
    y_sample = trunk(x_sample, ffn1_norm, ffn1_w_gate, ffn1_w_up, ffn1_w_down, mix_norm, w_in, q_norm, k_norm,
                     mlstm_gate_bias, mlstm_head_norm, w_up_att, w_up_mlstm, w_out,
                     ffn2_norm, ffn2_w_gate, ffn2_w_up, ffn2_w_down, final_norm)
    return (y_prompt, y_sample)
```

```python
import functools

import jax
import jax.numpy as jnp
import numpy as np
from jax import lax
from jax.experimental import pallas as pl
from jax.experimental.pallas import tpu as pltpu

D_MODEL = 1024
D_FF = 2816
GRID_W = 64
EPS = 1e-6
ATT_HEADS = 8
ATT_KV_HEADS = 2
ATT_HEAD_DIM = 64
ROPE_AXIS_DIM = ATT_HEAD_DIM // 2
ROPE_THETA = 10000.0
M_HEADS = 4
M_HEAD_DIM = 128
ATT_WIDTH = ATT_HEADS * ATT_HEAD_DIM
KV_WIDTH = ATT_KV_HEADS * ATT_HEAD_DIM
M_WIDTH = M_HEADS * M_HEAD_DIM
N_GATE_COLS = 4 * M_HEADS
IN_SPLITS = (ATT_WIDTH, KV_WIDTH, KV_WIDTH, M_WIDTH, M_WIDTH, M_WIDTH, N_GATE_COLS, M_WIDTH, D_MODEL, D_MODEL)

V7X_LANES = 128
V7X_SUBLANES = 8
V7X_VMEM_BYTES = 64 * 1024 * 1024

TOKEN_TILE = 512
ATT_Q_TILE = 256
M_CHUNK = 128

F32 = jnp.float32
BF16 = jnp.bfloat16
NT_DIMS = (((1,), (1,)), ((), ()))
TN_DIMS = (((0,), (0,)), ((), ()))


def _vmem_limit(nbytes):
    return int(min(nbytes * 1.25 + (4 << 20), V7X_VMEM_BYTES - (6 << 20)))


def _rms_norm(x, g):
    return x * lax.rsqrt(jnp.mean(x * x, axis=-1, keepdims=True) + EPS) * g


def _const_spec(shape):
    return pl.BlockSpec(shape, lambda *_: (0,) * len(shape), pipeline_mode=pl.Buffered(1))


def _ffn_body(x_ref, ng_ref, wg_ref, wu_ref, wd_ref, fin_ref, o_ref, *, apply_final_norm):
    x = x_ref[...]
    xn = _rms_norm(x, ng_ref[...]).astype(BF16)
    g = jnp.dot(xn, wg_ref[...], preferred_element_type=F32)
    u = jnp.dot(xn, wu_ref[...], preferred_element_type=F32)
    a = (g * jax.nn.sigmoid(g) * u).astype(BF16)
    y = x + 0.5 * jnp.dot(a, wd_ref[...], preferred_element_type=F32)
    if apply_final_norm:
        y = _rms_norm(y, fin_ref[...])
    o_ref[...] = y


def _ffn(x2d, norm_g, w_gate, w_up, w_down, final_g, apply_final_norm):
    T = x2d.shape[0]
    tm = TOKEN_TILE
    row = pl.BlockSpec((tm, D_MODEL), lambda i: (i, 0))
    est = (3 * D_MODEL * D_FF * 2 + 4 * tm * D_MODEL * 4 + tm * D_FF * (4 + 4 + 2) + tm * D_MODEL * 8)
    return pl.pallas_call(
        functools.partial(_ffn_body, apply_final_norm=apply_final_norm),
        grid=(T // tm,),
        in_specs=[row, _const_spec((1, D_MODEL)), _const_spec((D_MODEL, D_FF)), _const_spec((D_MODEL, D_FF)),
                  _const_spec((D_FF, D_MODEL)), _const_spec((1, D_MODEL))],
        out_specs=row,
        out_shape=jax.ShapeDtypeStruct((T, D_MODEL), F32),
        compiler_params=pltpu.CompilerParams(dimension_semantics=("parallel",), vmem_limit_bytes=_vmem_limit(est)),
        name="ffn",
    )(x2d, norm_g, w_gate, w_up, w_down, final_g)


def _norm_rope(x, gain, cos, sin_signed, lo, even):
    x2 = x * x
    ss_lo = jnp.sum(jnp.where(lo, x2, 0.0), axis=-1, keepdims=True)
    ss_hi = jnp.sum(jnp.where(lo, 0.0, x2), axis=-1, keepdims=True)
    inv = jnp.where(lo, lax.rsqrt(ss_lo / ATT_HEAD_DIM + EPS), lax.rsqrt(ss_hi / ATT_HEAD_DIM + EPS))
    y = x * inv * gain
    partner = jnp.where(even, pltpu.roll(y, V7X_LANES - 1, 1), pltpu.roll(y, 1, 1))
    return y * cos + partner * sin_signed


def _dup_halves(x, lo):
    xr = pltpu.roll(x, ATT_HEAD_DIM, 1)
    return jnp.where(lo, x, xr), jnp.where(lo, xr, x)


def _proj_body(x_ref, ng_ref, wqkv_ref, wm_ref, wgate_ref, wsg_ref, qg_ref, kg_ref, cos_ref, sin_ref,
               q_ref, kd_ref, vd_ref, m_ref, gate_ref, og_ref, ga_ref, gm_ref):
    h = _rms_norm(x_ref[...], ng_ref[...]).astype(BF16)
    tm = h.shape[0]
    lane = lax.broadcasted_iota(jnp.int32, (tm, V7X_LANES), 1)
    lo = lane < ATT_HEAD_DIM
    even = (lane & 1) == 0
    cos = cos_ref[...]
    sin = sin_ref[...]

    qkv = jnp.dot(h, wqkv_ref[...], preferred_element_type=F32)
    for j in range(ATT_WIDTH // V7X_LANES):
        sl = slice(j * V7X_LANES, (j + 1) * V7X_LANES)
        qj = _norm_rope(qkv[:, sl], qg_ref[...], cos, sin, lo, even)
        q_ref[:, sl] = (qj * (ATT_HEAD_DIM ** -0.5)).astype(BF16)
    k = _norm_rope(qkv[:, ATT_WIDTH:ATT_WIDTH + KV_WIDTH], kg_ref[...], cos, sin, lo, even)
    k0, k1 = _dup_halves(k, lo)
    kd_ref[:, :V7X_LANES] = k0.astype(BF16)
    kd_ref[:, V7X_LANES:] = k1.astype(BF16)
    v0, v1 = _dup_halves(qkv[:, ATT_WIDTH + KV_WIDTH:], lo)
    vd_ref[:, :V7X_LANES] = v0.astype(BF16)
    vd_ref[:, V7X_LANES:] = v1.astype(BF16)

    m = jnp.dot(h, wm_ref[...], preferred_element_type=F32)
    m_ref[:, :M_WIDTH] = m[:, :M_WIDTH].astype(BF16)
    m_ref[:, M_WIDTH:2 * M_WIDTH] = (m[:, M_WIDTH:2 * M_WIDTH] * (M_HEAD_DIM ** -0.5)).astype(BF16)
    m_ref[:, 2 * M_WIDTH:] = m[:, 2 * M_WIDTH:].astype(BF16)

    gate_ref[...] = jnp.dot(h, wgate_ref[...], preferred_element_type=F32)

    sg = jax.nn.sigmoid(jnp.dot(h, wsg_ref[...], preferred_element_type=F32))
    og_ref[...] = sg[:, :M_WIDTH].astype(BF16)
    ga_ref[...] = sg[:, M_WIDTH:M_WIDTH + D_MODEL].astype(BF16)
    gm_ref[...] = sg[:, M_WIDTH + D_MODEL:].astype(BF16)


def _proj(x2d, seq_len, norm_g, w_qkv, w_m, w_gate, w_sg, q_gain, k_gain, cos_t, sin_t):
    T = x2d.shape[0]
    tm = TOKEN_TILE
    tiles_per_seq = seq_len // tm
    n_qkv, n_m, n_sg = w_qkv.shape[1], w_m.shape[1], w_sg.shape[1]

    def row(width):
        return pl.BlockSpec((tm, width), lambda i: (i, 0))

    rope = pl.BlockSpec((tm, V7X_LANES), lambda i: (i % tiles_per_seq, 0))
    widths_bf16 = (ATT_WIDTH, 2 * KV_WIDTH, 2 * KV_WIDTH, 3 * M_WIDTH, M_WIDTH, D_MODEL, D_MODEL)
    est = (D_MODEL * (n_qkv + n_m + V7X_LANES + n_sg) * 2 + 2 * tm * D_MODEL * 4
           + tm * (n_qkv + n_m + n_sg) * 6 + 2 * tm * sum(widths_bf16) * 2 + 6 * tm * V7X_LANES * 4)
    return pl.pallas_call(
        _proj_body,
        grid=(T // tm,),
        in_specs=[row(D_MODEL), _const_spec((1, D_MODEL)), _const_spec((D_MODEL, n_qkv)), _const_spec((D_MODEL, n_m)),
                  _const_spec((D_MODEL, V7X_LANES)), _const_spec((D_MODEL, n_sg)),
                  _const_spec((1, V7X_LANES)), _const_spec((1, V7X_LANES)), rope, rope],
        out_specs=[row(w) for w in widths_bf16[:4]] + [row(V7X_LANES)] + [row(w) for w in widths_bf16[4:]],
        out_shape=[jax.ShapeDtypeStruct((T, w), BF16) for w in widths_bf16[:4]]
        + [jax.ShapeDtypeStruct((T, V7X_LANES), F32)]
        + [jax.ShapeDtypeStruct((T, w), BF16) for w in widths_bf16[4:]],
        compiler_params=pltpu.CompilerParams(dimension_semantics=("parallel",), vmem_limit_bytes=_vmem_limit(est)),
        name="proj",
    )(x2d, norm_g, w_qkv, w_m, w_gate, w_sg, q_gain, k_gain, cos_t, sin_t)


def _attn_body(q_ref, kd_ref, vd_ref, o_ref):
    tq = q_ref.shape[1]
    lane = lax.broadcasted_iota(jnp.int32, (tq, V7X_LANES), 1)
    lo = lane < ATT_HEAD_DIM
    pairs_per_kv = (ATT_HEADS // ATT_KV_HEADS) // 2
    for g in range(ATT_KV_HEADS):
        kd = kd_ref[0, :, g * V7X_LANES:(g + 1) * V7X_LANES]
        vd = vd_ref[0, :, g * V7X_LANES:(g + 1) * V7X_LANES]
        for p in range(pairs_per_kv):
            c0 = (g * pairs_per_kv + p) * V7X_LANES
            q2 = q_ref[0, :, c0:c0 + V7X_LANES]
            outs = []
            for keep in (lo, jnp.logical_not(lo)):
                qm = jnp.where(keep, q2, jnp.zeros_like(q2))
                s = lax.dot_general(qm, kd, NT_DIMS, preferred_element_type=F32)
                e = jnp.exp(s - jnp.max(s, axis=-1, keepdims=True))
                denom = jnp.sum(e, axis=-1, keepdims=True)
                o = jnp.dot(e.astype(BF16), vd, preferred_element_type=F32)
                outs.append(o / denom)
            o_ref[0, :, c0:c0 + V7X_LANES] = jnp.where(lo, outs[0], outs[1]).astype(BF16)


def _attention(q, kd, vd):
    B, S, _ = q.shape
    tq = ATT_Q_TILE
    est = (2 * tq * ATT_WIDTH * 2 * 2 + 2 * 2 * S * 2 * KV_WIDTH * 2 + 3 * tq * S * 4)
    return pl.pallas_call(
        _attn_body,
        grid=(B, S // tq),
        in_specs=[pl.BlockSpec((1, tq, ATT_WIDTH), lambda b, i: (b, i, 0)),
                  pl.BlockSpec((1, S, 2 * KV_WIDTH), lambda b, i: (b, 0, 0)),
                  pl.BlockSpec((1, S, 2 * KV_WIDTH), lambda b, i: (b, 0, 0))],
        out_specs=pl.BlockSpec((1, tq, ATT_WIDTH), lambda b, i: (b, i, 0)),
        out_shape=jax.ShapeDtypeStruct((B, S, ATT_WIDTH), BF16),
        compiler_params=pltpu.CompilerParams(dimension_semantics=("parallel", "parallel"),
                                             vmem_limit_bytes=_vmem_limit(est)),
        name="attn",
    )(q, kd, vd)


def _split3(x):
    hi = x.astype(BF16)
    r = x - hi.astype(F32)
    mid = r.astype(BF16)
    lo = (r - mid.astype(F32)).astype(BF16)
    return hi, mid, lo


def _log_sigmoid(x):
    return jnp.minimum(x, 0.0) - jnp.log1p(jnp.exp(-jnp.abs(x)))


def _mlstm_body(q_ref, k_ref, v_ref, gate_ref, bias_ref, og_ref, hg_ref, o_ref,
                dec_ref, li_ref, u_ref, un_ref, bl_ref, gx_ref, c_ref, n_ref, mm_ref, h_ref):
    S = q_ref.shape[1]
    L = M_CHUNK
    nc = S // L
    hd = pl.program_id(1)

    gates = gate_ref[0] + bias_ref[...]
    lane = lax.broadcasted_iota(jnp.int32, (S, V7X_LANES), 1)

    def column(j):
        c = jnp.sum(jnp.where(lane == j, gates, 0.0), axis=1, keepdims=True)
        return jnp.broadcast_to(c, (S, V7X_LANES))

    li_ref[0] = column(hd)
    dec_ref[0] = _log_sigmoid(column(M_HEADS + hd))
    li_ref[1] = column(2 * M_HEADS + hd)
    dec_ref[1] = _log_sigmoid(column(3 * M_HEADS + hd))

    row = lax.broadcasted_iota(jnp.int32, (L, L), 0)
    col = lax.broadcasted_iota(jnp.int32, (L, L), 1)

    for d in range(2):
        fwd = d == 0
        causal = (col <= row) if fwd else (col >= row)
        cum_lhs = jnp.where(causal, 1.0, 0.0).astype(BF16)

        def chunk_summary(c, carry, d=d, fwd=fwd, cum_lhs=cum_lhs):
            r0 = pl.multiple_of(c * L, L)
            lf = dec_ref[d, pl.ds(r0, L), :]
            li = li_ref[d, pl.ds(r0, L), :]
            hi, mid, lo3 = _split3(lf)
            b = (jnp.dot(cum_lhs, hi, preferred_element_type=F32) + jnp.dot(cum_lhs, mid, preferred_element_type=F32)
                 + jnp.dot(cum_lhs, lo3, preferred_element_type=F32))
            dec_ref[d, pl.ds(r0, L), :] = b
            b_end = b[L - 1:L, :] if fwd else b[0:1, :]
            g = b_end - b + li
            gmax = jnp.max(g, axis=0, keepdims=True)
            w = jnp.exp(g - gmax)
            kc = k_ref[0, pl.ds(r0, L), :].astype(F32)
            vc = v_ref[0, pl.ds(r0, L), :].astype(F32)
            u_ref[d, c] = lax.dot_general(kc.astype(BF16), (w * vc).astype(BF16), TN_DIMS, preferred_element_type=F32)
            un_ref[d, c] = jnp.broadcast_to(jnp.sum(w * kc, axis=0, keepdims=True), (V7X_SUBLANES, V7X_LANES))
            bl_ref[d, c] = jnp.broadcast_to(b_end, (V7X_SUBLANES, V7X_LANES))
            gx_ref[d, c] = jnp.broadcast_to(gmax, (V7X_SUBLANES, V7X_LANES))
            return carry

        lax.fori_loop(0, nc, chunk_summary, 0)

        def state_step(i, carry, d=d, fwd=fwd):
            C, n, m = carry
            c = i if fwd else nc - 1 - i
            c_ref[d, c] = C.astype(BF16)
            n_ref[d, c] = jnp.broadcast_to(n, (V7X_SUBLANES, V7X_LANES))
            mm_ref[d, c] = jnp.broadcast_to(m, (V7X_SUBLANES, V7X_LANES))
            b_end = bl_ref[d, c][0:1, :]
            gmax = gx_ref[d, c][0:1, :]
            m_new = jnp.maximum(b_end + m, gmax)
            keep = jnp.exp(b_end + m - m_new)
            add = jnp.exp(gmax - m_new)
            return keep * C + add * u_ref[d, c], keep * n + add * un_ref[d, c][0:1, :], m_new

        lax.fori_loop(0, nc, state_step,
                      (jnp.zeros((L, M_HEAD_DIM), F32), jnp.zeros((1, V7X_LANES), F32), jnp.zeros((1, V7X_LANES), F32)))

        def chunk_output(c, carry, d=d, causal=causal):
            r0 = pl.multiple_of(c * L, L)
            qc = q_ref[0, pl.ds(r0, L), :]
            kc = k_ref[0, pl.ds(r0, L), :]
            vc = v_ref[0, pl.ds(r0, L), :]
            b = dec_ref[d, pl.ds(r0, L), :]
            li = li_ref[d, pl.ds(r0, L), :]
            dmat = jnp.where(causal, b + (li - b).T, -jnp.inf)
            a = jnp.max(dmat, axis=1, keepdims=True)
            p = lax.dot_general(qc, kc, NT_DIMS, preferred_element_type=F32) * jnp.exp(dmat - a)
            rowsum = jnp.sum(p, axis=1, keepdims=True)
            intra = jnp.dot(p.astype(BF16), vc, preferred_element_type=F32)
            m_in = b[:, 0:1] + mm_ref[d, c][0:1, 0:1]
            m_t = jnp.maximum(a, m_in)
            wi = jnp.exp(a - m_t)
            wo = jnp.exp(m_in - m_t)
            inter = jnp.dot(qc, c_ref[d, c], preferred_element_type=F32)
            qn = jnp.sum(qc.astype(F32) * n_ref[d, c][0:1, :], axis=1, keepdims=True)
            num = wi * intra + wo * inter
            den = wi * rowsum + wo * qn
            h_ref[d, pl.ds(r0, L), :] = num / jnp.maximum(jnp.abs(den), jnp.exp(-m_t))
            return carry

        lax.fori_loop(0, nc, chunk_output, 0)

    h = h_ref[0] + h_ref[1]
    hn = h * lax.rsqrt(jnp.mean(h * h, axis=-1, keepdims=True) + EPS) * hg_ref[...]
    o_ref[0] = (og_ref[0].astype(F32) * hn).astype(BF16)


def _mlstm(m, gates, gate_bias, og, head_gain):
    B, S, _ = m.shape
    L = M_CHUNK
    nc = S // L
    d = M_HEAD_DIM
    tile = (V7X_SUBLANES, V7X_LANES)

    def head_block(offset):
        return pl.BlockSpec((1, S, d), lambda b, h: (b, 0, offset + h))

    scratch = [pltpu.VMEM((2, S, V7X_LANES), F32), pltpu.VMEM((2, S, V7X_LANES), F32),
               pltpu.VMEM((2, nc, d, d), F32), pltpu.VMEM((2, nc) + tile, F32),
               pltpu.VMEM((2, nc) + tile, F32), pltpu.VMEM((2, nc) + tile, F32),
               pltpu.VMEM((2, nc, d, d), BF16), pltpu.VMEM((2, nc) + tile, F32),
               pltpu.VMEM((2, nc) + tile, F32), pltpu.VMEM((2, S, d), F32)]
    est = (2 * (3 * S * d * 2 + S * V7X_LANES * 4 + 2 * S * d * 2) + 6 * S * V7X_LANES * 4
           + 2 * nc * d * d * 6 + 5 * 2 * nc * 4096 + 8 * S * V7X_LANES * 4)
    return pl.pallas_call(
        _mlstm_body,
        grid=(B, M_HEADS),
        in_specs=[head_block(0), head_block(M_HEADS), head_block(2 * M_HEADS),
                  pl.BlockSpec((1, S, V7X_LANES), lambda b, h: (b, 0, 0)),
                  pl.BlockSpec((1, V7X_LANES), lambda b, h: (0, 0)),
                  head_block(0),
                  pl.BlockSpec((1, d), lambda b, h: (0, h))],
        out_specs=head_block(0),
        out_shape=jax.ShapeDtypeStruct((B, S, M_WIDTH), BF16),
        scratch_shapes=scratch,
        compiler_params=pltpu.CompilerParams(dimension_semantics=("parallel", "arbitrary"),
                                             vmem_limit_bytes=_vmem_limit(est)),
        name="mlstm",
    )(m, m, m, gates, gate_bias, og, head_gain)


def _merge_body(x_ref, att_ref, mem_ref, ga_ref, gm_ref, wa_ref, wm_ref, wo_ref, o_ref):
    ya = jnp.dot(att_ref[...], wa_ref[...], preferred_element_type=F32)
    ym = jnp.dot(mem_ref[...], wm_ref[...], preferred_element_type=F32)
    merged = (ga_ref[...].astype(F32) * ya + gm_ref[...].astype(F32) * ym).astype(BF16)
    o_ref[...] = x_ref[...] + jnp.dot(merged, wo_ref[...], preferred_element_type=F32)


def _merge(x2d, att, mem, ga, gm, w_ua, w_um, w_out):
    T = x2d.shape[0]
    tm = TOKEN_TILE

    def row(width):
        return pl.BlockSpec((tm, width), lambda i: (i, 0))

    est = ((ATT_WIDTH + M_WIDTH + D_MODEL) * D_MODEL * 2 + 4 * tm * D_MODEL * 4
           + 2 * tm * (ATT_WIDTH + M_WIDTH + 2 * D_MODEL) * 2 + 4 * tm * D_MODEL * 4)
    return pl.pallas_call(
        _merge_body,
        grid=(T // tm,),
        in_specs=[row(D_MODEL), row(ATT_WIDTH), row(M_WIDTH), row(D_MODEL), row(D_MODEL),
                  _const_spec((ATT_WIDTH, D_MODEL)), _const_spec((M_WIDTH, D_MODEL)), _const_spec((D_MODEL, D_MODEL))],
        out_specs=row(D_MODEL),
        out_shape=jax.ShapeDtypeStruct((T, D_MODEL), F32),
        compiler_params=pltpu.CompilerParams(dimension_semantics=("parallel",), vmem_limit_bytes=_vmem_limit(est)),
        name="merge",
    )(x2d, att, mem, ga, gm, w_ua, w_um, w_out)


def _rope_tables(seq_len):
    t = np.arange(seq_len)
    inv_freq = ROPE_THETA ** (-np.arange(0, ROPE_AXIS_DIM, 2, dtype=np.float32) / ROPE_AXIS_DIM)
    ang = np.concatenate([(t // GRID_W).astype(np.float32)[:, None] * inv_freq,
                          (t % GRID_W).astype(np.float32)[:, None] * inv_freq], axis=-1).astype(np.float32)
    ang = jnp.asarray(ang)
    cos = jnp.repeat(jnp.cos(ang), 2, axis=-1)
    sin = jnp.repeat(jnp.sin(ang), 2, axis=-1) * jnp.tile(jnp.asarray([-1.0, 1.0], F32), ATT_HEAD_DIM // 2)
    return jnp.tile(cos, (1, 2)), jnp.tile(sin, (1, 2))


def _layer_params(l, ffn1_norm, ffn1_w_gate, ffn1_w_up, ffn1_w_down, mix_norm, w_in, q_norm, k_norm,
                  mlstm_gate_bias, mlstm_head_norm, w_up_att, w_up_mlstm, w_out,
                  ffn2_norm, ffn2_w_gate, ffn2_w_up, ffn2_w_down):
    offs = np.concatenate([[0], np.cumsum(IN_SPLITS)])
    w = w_in[l].astype(BF16)
    seg = [w[:, offs[i]:offs[i + 1]] for i in range(len(IN_SPLITS))]
    pad = V7X_LANES - N_GATE_COLS
    return dict(
        ffn1=(ffn1_norm[l][None], ffn1_w_gate[l].astype(BF16), ffn1_w_up[l].astype(BF16), ffn1_w_down[l].astype(BF16)),
        ffn2=(ffn2_norm[l][None], ffn2_w_gate[l].astype(BF16), ffn2_w_up[l].astype(BF16), ffn2_w_down[l].astype(BF16)),
        mix_norm=mix_norm[l][None],
        w_qkv=jnp.concatenate(seg[0:3], axis=1),
        w_m=jnp.concatenate(seg[3:6], axis=1),
        w_gate=jnp.pad(seg[6], ((0, 0), (0, pad))),
        w_sg=jnp.concatenate(seg[7:10], axis=1),
        q_gain=jnp.tile(q_norm[l], 2)[None],
        k_gain=jnp.tile(k_norm[l], 2)[None],
        gate_bias=jnp.pad(mlstm_gate_bias[l], (0, pad))[None],
        head_gain=mlstm_head_norm[l][None],
        w_ua=w_up_att[l].astype(BF16), w_um=w_up_mlstm[l].astype(BF16), w_out=w_out[l].astype(BF16),
    )


def _trunk(x, layers, final_g):
    B, S, D = x.shape
    cos_t, sin_t = _rope_tables(S)
    x2d = x.reshape(B * S, D)
    for i, p in enumerate(layers):
        last = i == len(layers) - 1
        x1 = _ffn(x2d, *p["ffn1"], final_g, False)
        q, kd, vd, m, gates, og, ga, gm = _proj(x1, S, p["mix_norm"], p["w_qkv"], p["w_m"], p["w_gate"], p["w_sg"],
                                                p["q_gain"], p["k_gain"], cos_t, sin_t)
        att = _attention(q.reshape(B, S, -1), kd.reshape(B, S, -1), vd.reshape(B, S, -1))
        mem = _mlstm(m.reshape(B, S, -1), gates.reshape(B, S, -1), p["gate_bias"], og.reshape(B, S, -1), p["head_gain"])
        x2 = _merge(x1, att.reshape(B * S, -1), mem.reshape(B * S, -1), ga, gm, p["w_ua"], p["w_um"], p["w_out"])
        x2d = _ffn(x2, *p["ffn2"], final_g, last)
        if last:
            return x2d.reshape(B, S, D)
    return _rms_norm(x2d, final_g).reshape(B, S, D)


def kernel(x_prompt, x_sample, ffn1_norm, ffn1_w_gate, ffn1_w_up, ffn1_w_down, mix_norm, w_in, q_norm, k_norm,
           mlstm_gate_bias, mlstm_head_norm, w_up_att, w_up_mlstm, w_out,
           ffn2_norm, ffn2_w_gate, ffn2_w_up, ffn2_w_down, final_norm):
    depth = w_in.shape[0]
    layers = [_layer_params(l, ffn1_norm, ffn1_w_gate, ffn1_w_up, ffn1_w_down, mix_norm, w_in, q_norm, k_norm,
                            mlstm_gate_bias, mlstm_head_norm, w_up_att, w_up_mlstm, w_out,
                            ffn2_norm, ffn2_w_gate, ffn2_w_up, ffn2_w_down) for l in range(depth)]
    final_g = final_norm[None]
    return _trunk(x_prompt, layers, final_g), _trunk(x_sample, layers, final_g)
```

```python
import functools

import jax
import jax.numpy as jnp
import numpy as np
from jax import lax
from jax.experimental import pallas as pl
from jax.experimental.pallas import tpu as pltpu

D_MODEL = 1024
D_FF = 2816
GRID_W = 64
EPS = 1e-6
ATT_HEADS = 8
ATT_KV_HEADS = 2
ATT_HEAD_DIM = 64
ROPE_AXIS_DIM = ATT_HEAD_DIM // 2
ROPE_THETA = 10000.0
M_HEADS = 4
M_HEAD_DIM = 128
ATT_WIDTH = ATT_HEADS * ATT_HEAD_DIM
KV_WIDTH = ATT_KV_HEADS * ATT_HEAD_DIM
M_WIDTH = M_HEADS * M_HEAD_DIM
N_GATE_COLS = 4 * M_HEADS
IN_SPLITS = (ATT_WIDTH, KV_WIDTH, KV_WIDTH, M_WIDTH, M_WIDTH, M_WIDTH, N_GATE_COLS, M_WIDTH, D_MODEL, D_MODEL)

V7X_LANES = 128
V7X_SUBLANES = 8
V7X_VMEM_BYTES = 64 * 1024 * 1024

TOKEN_TILE = 512
ATT_Q_TILE = 256
M_CHUNK = 128

F32 = jnp.float32
BF16 = jnp.bfloat16
NT_DIMS = (((1,), (1,)), ((), ()))
TN_DIMS = (((0,), (0,)), ((), ()))


def _vmem_limit(nbytes):
    return int(min(nbytes * 1.25 + (4 << 20), V7X_VMEM_BYTES - (6 << 20)))


def _rms_norm(x, g):
    return x * lax.rsqrt(jnp.mean(x * x, axis=-1, keepdims=True) + EPS) * g


def _const_spec(shape):
    return pl.BlockSpec(shape, lambda *_: (0,) * len(shape), pipeline_mode=pl.Buffered(1))


def _ffn_body(x_ref, ng_ref, wg_ref, wu_ref, wd_ref, fin_ref, o_ref, *, apply_final_norm):
    x = x_ref[...]
    xn = _rms_norm(x, ng_ref[...]).astype(BF16)
    g = jnp.dot(xn, wg_ref[...], preferred_element_type=F32)
    u = jnp.dot(xn, wu_ref[...], preferred_element_type=F32)
    a = (g * jax.nn.sigmoid(g) * u).astype(BF16)
    y = x + 0.5 * jnp.dot(a, wd_ref[...], preferred_element_type=F32)
    if apply_final_norm:
        y = _rms_norm(y, fin_ref[...])
    o_ref[...] = y


def _ffn(x2d, norm_g, w_gate, w_up, w_down, final_g, apply_final_norm):
    T = x2d.shape[0]
    tm = TOKEN_TILE
    row = pl.BlockSpec((tm, D_MODEL), lambda i: (i, 0))
    est = (3 * D_MODEL * D_FF * 2 + 4 * tm * D_MODEL * 4 + tm * D_FF * (4 + 4 + 2) + tm * D_MODEL * 8)
    return pl.pallas_call(
        functools.partial(_ffn_body, apply_final_norm=apply_final_norm),
        grid=(T // tm,),
        in_specs=[row, _const_spec((1, D_MODEL)), _const_spec((D_MODEL, D_FF)), _const_spec((D_MODEL, D_FF)),
                  _const_spec((D_FF, D_MODEL)), _const_spec((1, D_MODEL))],
        out_specs=row,
        out_shape=jax.ShapeDtypeStruct((T, D_MODEL), F32),
        compiler_params=pltpu.CompilerParams(dimension_semantics=("parallel",), vmem_limit_bytes=_vmem_limit(est)),
        name="ffn",
    )(x2d, norm_g, w_gate, w_up, w_down, final_g)


def _norm_rope(x, gain, cos, sin_signed, lo, even):
    x2 = x * x
    ss_lo = jnp.sum(jnp.where(lo, x2, 0.0), axis=-1, keepdims=True)
    ss_hi = jnp.sum(jnp.where(lo, 0.0, x2), axis=-1, keepdims=True)
    inv = jnp.where(lo, lax.rsqrt(ss_lo / ATT_HEAD_DIM + EPS), lax.rsqrt(ss_hi / ATT_HEAD_DIM + EPS))
    y = x * inv * gain
    partner = jnp.where(even, pltpu.roll(y, V7X_LANES - 1, 1), pltpu.roll(y, 1, 1))
    return y * cos + partner * sin_signed


def _dup_halves(x, lo):
    xr = pltpu.roll(x, ATT_HEAD_DIM, 1)
    return jnp.where(lo, x, xr), jnp.where(lo, xr, x)


def _proj_body(x_ref, ng_ref, wqkv_ref, wm_ref, wt_ref, wsg_ref, qg_ref, kg_ref, cos_ref, sin_ref, bias_ref,
               q_ref, kd_ref, vd_ref, mqk_ref, mvt_ref, gt_ref, og_ref, ga_ref, gm_ref):
    h = _rms_norm(x_ref[...], ng_ref[...]).astype(BF16)
    tm = h.shape[0]
    lane = lax.broadcasted_iota(jnp.int32, (tm, V7X_LANES), 1)
    lo = lane < ATT_HEAD_DIM
    even = (lane & 1) == 0
    cos = cos_ref[...]
    sin = sin_ref[...]

    qkv = jnp.dot(h, wqkv_ref[...], preferred_element_type=F32)
    for j in range(ATT_WIDTH // V7X_LANES):
        sl = slice(j * V7X_LANES, (j + 1) * V7X_LANES)
        qj = _norm_rope(qkv[:, sl], qg_ref[...], cos, sin, lo, even)
        q_ref[:, sl] = (qj * (ATT_HEAD_DIM ** -0.5)).astype(BF16)
    k = _norm_rope(qkv[:, ATT_WIDTH:ATT_WIDTH + KV_WIDTH], kg_ref[...], cos, sin, lo, even)
    k0, k1 = _dup_halves(k, lo)
    kd_ref[:, :V7X_LANES] = k0.astype(BF16)
    kd_ref[:, V7X_LANES:] = k1.astype(BF16)
    v0, v1 = _dup_halves(qkv[:, ATT_WIDTH + KV_WIDTH:], lo)
    vd_ref[:, :V7X_LANES] = v0.astype(BF16)
    vd_ref[:, V7X_LANES:] = v1.astype(BF16)

    m = jnp.dot(h, wm_ref[...], preferred_element_type=F32)
    mqk_ref[:, :M_WIDTH] = m[:, :M_WIDTH].astype(BF16)
    mqk_ref[:, M_WIDTH:] = (m[:, M_WIDTH:] * (M_HEAD_DIM ** -0.5)).astype(BF16)

    t = lax.dot_general(wt_ref[...], h, NT_DIMS, preferred_element_type=F32)
    for j in range(tm // M_CHUNK):
        mvt_ref[j] = t[:M_WIDTH, j * M_CHUNK:(j + 1) * M_CHUNK].astype(BF16)
    gt_ref[...] = t[M_WIDTH:, :] + bias_ref[...]

    sg = jax.nn.sigmoid(jnp.dot(h, wsg_ref[...], preferred_element_type=F32))
    og_ref[...] = sg[:, :M_WIDTH].astype(BF16)
    ga_ref[...] = sg[:, M_WIDTH:M_WIDTH + D_MODEL].astype(BF16)
    gm_ref[...] = sg[:, M_WIDTH + D_MODEL:].astype(BF16)


def _proj(x2d, seq_len, norm_g, w_qkv, w_m, w_t, w_sg, q_gain, k_gain, cos_t, sin_t, gate_bias):
    T = x2d.shape[0]
    tm = TOKEN_TILE
    tiles_per_seq = seq_len // tm
    n_qkv, n_m, n_t, n_sg = w_qkv.shape[1], w_m.shape[1], w_t.shape[0], w_sg.shape[1]

    def row(width):
        return pl.BlockSpec((tm, width), lambda i: (i, 0))

    rope = pl.BlockSpec((tm, V7X_LANES), lambda i: (i % tiles_per_seq, 0))
    outs = [
        (jax.ShapeDtypeStruct((T, ATT_WIDTH), BF16), row(ATT_WIDTH)),
        (jax.ShapeDtypeStruct((T, 2 * KV_WIDTH), BF16), row(2 * KV_WIDTH)),
        (jax.ShapeDtypeStruct((T, 2 * KV_WIDTH), BF16), row(2 * KV_WIDTH)),
        (jax.ShapeDtypeStruct((T, 2 * M_WIDTH), BF16), row(2 * M_WIDTH)),
        (jax.ShapeDtypeStruct((T // M_CHUNK, M_WIDTH, M_CHUNK), BF16),
         pl.BlockSpec((tm // M_CHUNK, M_WIDTH, M_CHUNK), lambda i: (i, 0, 0))),
        (jax.ShapeDtypeStruct((N_GATE_COLS, T), F32), pl.BlockSpec((N_GATE_COLS, tm), lambda i: (0, i))),
        (jax.ShapeDtypeStruct((T, M_WIDTH), BF16), row(M_WIDTH)),
        (jax.ShapeDtypeStruct((T, D_MODEL), BF16), row(D_MODEL)),
        (jax.ShapeDtypeStruct((T, D_MODEL), BF16), row(D_MODEL)),
    ]
    n_out_bf16 = ATT_WIDTH + 4 * KV_WIDTH + 4 * M_WIDTH + 2 * D_MODEL
    est = (D_MODEL * (n_qkv + n_m + n_t + n_sg) * 2 + 2 * tm * D_MODEL * 4
           + tm * (n_qkv + n_m + n_t + n_sg) * 6 + 2 * tm * n_out_bf16 * 2 + 8 * tm * V7X_LANES * 4)
    return pl.pallas_call(
        _proj_body,
        grid=(T // tm,),
        in_specs=[row(D_MODEL), _const_spec((1, D_MODEL)), _const_spec((D_MODEL, n_qkv)), _const_spec((D_MODEL, n_m)),
                  _const_spec((n_t, D_MODEL)), _const_spec((D_MODEL, n_sg)),
                  _const_spec((1, V7X_LANES)), _const_spec((1, V7X_LANES)), rope, rope,
                  _const_spec((N_GATE_COLS, tm))],
        out_specs=[s for _, s in outs],
        out_shape=[o for o, _ in outs],
        compiler_params=pltpu.CompilerParams(dimension_semantics=("parallel",), vmem_limit_bytes=_vmem_limit(est)),
        name="proj",
    )(x2d, norm_g, w_qkv, w_m, w_t, w_sg, q_gain, k_gain, cos_t, sin_t, gate_bias)


def _attn_body(q_ref, kd_ref, vd_ref, o_ref):
    tq = q_ref.shape[1]
    lane = lax.broadcasted_iota(jnp.int32, (tq, V7X_LANES), 1)
    lo = lane < ATT_HEAD_DIM
    pairs_per_kv = (ATT_HEADS // ATT_KV_HEADS) // 2
    for g in range(ATT_KV_HEADS):
        kd = kd_ref[0, :, g * V7X_LANES:(g + 1) * V7X_LANES]
        vd = vd_ref[0, :, g * V7X_LANES:(g + 1) * V7X_LANES]
        for p in range(pairs_per_kv):
            c0 = (g * pairs_per_kv + p) * V7X_LANES
            q2 = q_ref[0, :, c0:c0 + V7X_LANES]
            outs = []
            for keep in (lo, jnp.logical_not(lo)):
                qm = jnp.where(keep, q2, jnp.zeros_like(q2))
                s = lax.dot_general(qm, kd, NT_DIMS, preferred_element_type=F32)
                e = jnp.exp(s - jnp.max(s, axis=-1, keepdims=True))
                denom = jnp.sum(e, axis=-1, keepdims=True)
                o = jnp.dot(e.astype(BF16), vd, preferred_element_type=F32)
                outs.append(o / denom)
            o_ref[0, :, c0:c0 + V7X_LANES] = jnp.where(lo, outs[0], outs[1]).astype(BF16)


def _attention(q, kd, vd):
    B, S, _ = q.shape
    tq = ATT_Q_TILE
    est = (2 * tq * ATT_WIDTH * 2 * 2 + 2 * 2 * S * 2 * KV_WIDTH * 2 + 3 * tq * S * 4)
    return pl.pallas_call(
        _attn_body,
        grid=(B, S // tq),
        in_specs=[pl.BlockSpec((1, tq, ATT_WIDTH), lambda b, i: (b, i, 0)),
                  pl.BlockSpec((1, S, 2 * KV_WIDTH), lambda b, i: (b, 0, 0)),
                  pl.BlockSpec((1, S, 2 * KV_WIDTH), lambda b, i: (b, 0, 0))],
        out_specs=pl.BlockSpec((1, tq, ATT_WIDTH), lambda b, i: (b, i, 0)),
        out_shape=jax.ShapeDtypeStruct((B, S, ATT_WIDTH), BF16),
        compiler_params=pltpu.CompilerParams(dimension_semantics=("parallel", "parallel"),
                                             vmem_limit_bytes=_vmem_limit(est)),
        name="attn",
    )(q, kd, vd)


def _split3(x):
    hi = x.astype(BF16)
    r = x - hi.astype(F32)
    mid = r.astype(BF16)
    lo = (r - mid.astype(F32)).astype(BF16)
    return hi, mid, lo


def _log_sigmoid(x):
    return jnp.minimum(x, 0.0) - jnp.log1p(jnp.exp(-jnp.abs(x)))


M_AUG = M_HEAD_DIM + 16


def _mlstm_body(q_ref, k_ref, vt_ref, g_ref, og_ref, hg_ref, o_ref,
                b_ref, x_ref, w_ref, bl_ref, gx_ref, ht_ref):
    L = M_CHUNK
    nc = vt_ref.shape[0]
    hd = pl.program_id(1)
    src = lax.broadcasted_iota(jnp.int32, (L, L), 0)
    dst = lax.broadcasted_iota(jnp.int32, (L, L), 1)
    visible = (src <= dst, src >= dst)

    for d in range(2):
        fwd = d == 0
        li = g_ref[2 * d * M_HEADS + hd]
        lf = _log_sigmoid(g_ref[(2 * d + 1) * M_HEADS + hd])
        tri = jnp.where(visible[d], 1.0, 0.0).astype(BF16)
        hi, mid, lo3 = _split3(lf)
        b = (jnp.dot(hi, tri, preferred_element_type=F32) + jnp.dot(mid, tri, preferred_element_type=F32)
             + jnp.dot(lo3, tri, preferred_element_type=F32))
        b_end = b[:, L - 1:L] if fwd else b[:, 0:1]
        g = b_end - b + li
        gmax = jnp.max(g, axis=1, keepdims=True)
        b_ref[d] = b
        x_ref[d] = li - b
        w_ref[d] = jnp.exp(g - gmax)
        bl_ref[d] = jnp.broadcast_to(b_end, (nc, L))
        gx_ref[d] = jnp.broadcast_to(gmax, (nc, L))

    ones_rows = jnp.where(lax.broadcasted_iota(jnp.int32, (M_AUG - M_HEAD_DIM, L), 0) == 0, 1.0, 0.0).astype(BF16)

    def step(i, carry):
        out = []
        for d in range(2):
            ct, m = carry[d]
            c = i if d == 0 else nc - 1 - i
            r0 = pl.multiple_of(c * L, L)
            qc = q_ref[0, pl.ds(r0, L), :]
            kc = k_ref[0, pl.ds(r0, L), :]
            vta = jnp.concatenate([vt_ref[c], ones_rows], axis=0)
            brow = b_ref[d, pl.ds(c, 1), :]
            xrow = x_ref[d, pl.ds(c, 1), :]
            wrow = w_ref[d, pl.ds(c, 1), :]
            b_end = bl_ref[d, pl.ds(c, 1), :]
            gmax = gx_ref[d, pl.ds(c, 1), :]

            ut = jnp.dot((vta.astype(F32) * wrow).astype(BF16), kc, preferred_element_type=F32)

            both = lax.dot_general(jnp.concatenate([kc, ct.astype(BF16)], axis=0), qc, NT_DIMS,
                                   preferred_element_type=F32)
            xcol = jnp.broadcast_to(xrow, (L, L)).T
            dmat = jnp.where(visible[d], xcol + brow, -jnp.inf)
            a = jnp.max(dmat, axis=0, keepdims=True)
            pt = both[:L] * jnp.exp(dmat - a)
            intra = jnp.dot(vta, pt.astype(BF16), preferred_element_type=F32)
            m_in = brow + m
            m_t = jnp.maximum(a, m_in)
            comb = jnp.exp(a - m_t) * intra + jnp.exp(m_in - m_t) * both[L:]
            den = comb[M_HEAD_DIM:M_HEAD_DIM + 1]
            ht_ref[d, c] = comb[:M_HEAD_DIM] * (1.0 / jnp.maximum(jnp.abs(den), jnp.exp(-m_t)))

            m_new = jnp.maximum(b_end + m, gmax)
            out.append((jnp.exp(b_end + m - m_new) * ct + jnp.exp(gmax - m_new) * ut, m_new))
        return tuple(out)

    init = (jnp.zeros((M_AUG, M_HEAD_DIM), F32), jnp.zeros((1, L), F32))
    lax.fori_loop(0, nc, step, (init, init), unroll=2)

    gain = hg_ref[...]

    def finish(c, carry):
        r0 = pl.multiple_of(c * L, L)
        ht = ht_ref[0, c] + ht_ref[1, c]
        inv = lax.rsqrt(jnp.mean(ht * ht, axis=0, keepdims=True) + EPS)
        hn = (ht * inv * gain).T
        o_ref[0, pl.ds(r0, L), :] = (og_ref[0, pl.ds(r0, L), :].astype(F32) * hn).astype(BF16)
        return carry

    lax.fori_loop(0, nc, finish, 0, unroll=2)


def _mlstm(mqk, mvt, gates_t, og, head_gain):
    B, S, _ = mqk.shape
    L = M_CHUNK
    nc = S // L
    d = M_HEAD_DIM

    def head_block(offset):
        return pl.BlockSpec((1, S, d), lambda b, h: (b, 0, offset + h))

    scratch = [pltpu.VMEM((2, nc, L), F32) for _ in range(5)] + [pltpu.VMEM((2, nc, d, L), F32)]
    est = 2 * (2 * S * d * 2 + nc * d * L * 2 + N_GATE_COLS * nc * L * 4 + 2 * S * d * 2) + 3 * nc * d * L * 4
    return pl.pallas_call(
        _mlstm_body,
        grid=(B, M_HEADS),
        in_specs=[head_block(0), head_block(M_HEADS),
                  pl.BlockSpec((nc, d, L), lambda b, h: (b, h, 0)),
                  pl.BlockSpec((N_GATE_COLS, nc, L), lambda b, h: (0, b, 0)),
                  head_block(0),
                  pl.BlockSpec((d, V7X_LANES), lambda b, h: (h, 0))],
        out_specs=head_block(0),
        out_shape=jax.ShapeDtypeStruct((B, S, M_WIDTH), BF16),
        scratch_shapes=scratch,
        compiler_params=pltpu.CompilerParams(dimension_semantics=("parallel", "arbitrary"),
                                             vmem_limit_bytes=_vmem_limit(est)),
        name="mlstm",
    )(mqk, mqk, mvt, gates_t, og, head_gain)


def _merge_body(x_ref, att_ref, mem_ref, ga_ref, gm_ref, wa_ref, wm_ref, wo_ref, o_ref):
    ya = jnp.dot(att_ref[...], wa_ref[...], preferred_element_type=F32)
    ym = jnp.dot(mem_ref[...], wm_ref[...], preferred_element_type=F32)
    merged = (ga_ref[...].astype(F32) * ya + gm_ref[...].astype(F32) * ym).astype(BF16)
    o_ref[...] = x_ref[...] + jnp.dot(merged, wo_ref[...], preferred_element_type=F32)


def _merge(x2d, att, mem, ga, gm, w_ua, w_um, w_out):
    T = x2d.shape[0]
    tm = TOKEN_TILE

    def row(width):
        return pl.BlockSpec((tm, width), lambda i: (i, 0))

    est = ((ATT_WIDTH + M_WIDTH + D_MODEL) * D_MODEL * 2 + 4 * tm * D_MODEL * 4
           + 2 * tm * (ATT_WIDTH + M_WIDTH + 2 * D_MODEL) * 2 + 4 * tm * D_MODEL * 4)
    return pl.pallas_call(
        _merge_body,
        grid=(T // tm,),
        in_specs=[row(D_MODEL), row(ATT_WIDTH), row(M_WIDTH), row(D_MODEL), row(D_MODEL),
                  _const_spec((ATT_WIDTH, D_MODEL)), _const_spec((M_WIDTH, D_MODEL)), _const_spec((D_MODEL, D_MODEL))],
        out_specs=row(D_MODEL),
        out_shape=jax.ShapeDtypeStruct((T, D_MODEL), F32),
        compiler_params=pltpu.CompilerParams(dimension_semantics=("parallel",), vmem_limit_bytes=_vmem_limit(est)),
        name="merge",
    )(x2d, att, mem, ga, gm, w_ua, w_um, w_out)


def _rope_tables(seq_len):
    t = np.arange(seq_len)
    inv_freq = ROPE_THETA ** (-np.arange(0, ROPE_AXIS_DIM, 2, dtype=np.float32) / ROPE_AXIS_DIM)
    ang = np.concatenate([(t // GRID_W).astype(np.float32)[:, None] * inv_freq,
                          (t % GRID_W).astype(np.float32)[:, None] * inv_freq], axis=-1).astype(np.float32)
    ang = jnp.asarray(ang)
    cos = jnp.repeat(jnp.cos(ang), 2, axis=-1)
    sin = jnp.repeat(jnp.sin(ang), 2, axis=-1) * jnp.tile(jnp.asarray([-1.0, 1.0], F32), ATT_HEAD_DIM // 2)
    return jnp.tile(cos, (1, 2)), jnp.tile(sin, (1, 2))


def _layer_params(l, ffn1_norm, ffn1_w_gate, ffn1_w_up, ffn1_w_down, mix_norm, w_in, q_norm, k_norm,
                  mlstm_gate_bias, mlstm_head_norm, w_up_att, w_up_mlstm, w_out,
                  ffn2_norm, ffn2_w_gate, ffn2_w_up, ffn2_w_down):
    offs = np.concatenate([[0], np.cumsum(IN_SPLITS)])
    w = w_in[l].astype(BF16)
    seg = [w[:, offs[i]:offs[i + 1]] for i in range(len(IN_SPLITS))]
    return dict(
        ffn1=(ffn1_norm[l][None], ffn1_w_gate[l].astype(BF16), ffn1_w_up[l].astype(BF16), ffn1_w_down[l].astype(BF16)),
        ffn2=(ffn2_norm[l][None], ffn2_w_gate[l].astype(BF16), ffn2_w_up[l].astype(BF16), ffn2_w_down[l].astype(BF16)),
        mix_norm=mix_norm[l][None],
        w_qkv=jnp.concatenate(seg[0:3], axis=1),
        w_m=jnp.concatenate(seg[3:5], axis=1),
        w_t=jnp.concatenate([seg[5].T, seg[6].T], axis=0),
        w_sg=jnp.concatenate(seg[7:10], axis=1),
        q_gain=jnp.tile(q_norm[l], 2)[None],
        k_gain=jnp.tile(k_norm[l], 2)[None],
        gate_bias=jnp.broadcast_to(mlstm_gate_bias[l][:, None], (N_GATE_COLS, TOKEN_TILE)),
        head_gain=jnp.broadcast_to(mlstm_head_norm[l][:, None], (M_WIDTH, V7X_LANES)),
        w_ua=w_up_att[l].astype(BF16), w_um=w_up_mlstm[l].astype(BF16), w_out=w_out[l].astype(BF16),
    )


def _trunk(x, layers, final_g):
    B, S, D = x.shape
    cos_t, sin_t = _rope_tables(S)
    x2d = x.reshape(B * S, D)
    for i, p in enumerate(layers):
        last = i == len(layers) - 1
        x1 = _ffn(x2d, *p["ffn1"], final_g, False)
        q, kd, vd, mqk, mvt, gates_t, og, ga, gm = _proj(x1, S, p["mix_norm"], p["w_qkv"], p["w_m"], p["w_t"], p["w_sg"],
                                                         p["q_gain"], p["k_gain"], cos_t, sin_t, p["gate_bias"])
        att = _attention(q.reshape(B, S, -1), kd.reshape(B, S, -1), vd.reshape(B, S, -1))
        mem = _mlstm(mqk.reshape(B, S, -1), mvt, gates_t.reshape(N_GATE_COLS, -1, M_CHUNK), og.reshape(B, S, -1),
                     p["head_gain"])
        x2 = _merge(x1, att.reshape(B * S, -1), mem.reshape(B * S, -1), ga, gm, p["w_ua"], p["w_um"], p["w_out"])
        x2d = _ffn(x2, *p["ffn2"], final_g, last)
        if last:
            return x2d.reshape(B, S, D)
    return _rms_norm(x2d, final_g).reshape(B, S, D)


def kernel(x_prompt, x_sample, ffn1_norm, ffn1_w_gate, ffn1_w_up, ffn1_w_down, mix_norm, w_in, q_norm, k_norm,
           mlstm_gate_bias, mlstm_head_norm, w_up_att, w_up_mlstm, w_out,
           ffn2_norm, ffn2_w_gate, ffn2_w_up, ffn2_w_down, final_norm):
    depth = w_in.shape[0]
    layers = [_layer_params(l, ffn1_norm, ffn1_w_gate, ffn1_w_up, ffn1_w_down, mix_norm, w_in, q_norm, k_norm,
                            mlstm_gate_bias, mlstm_head_norm, w_up_att, w_up_mlstm, w_out,
                            ffn2_norm, ffn2_w_gate, ffn2_w_up, ffn2_w_down) for l in range(depth)]
    final_g = final_norm[None]
    return _trunk(x_prompt, layers, final_g), _trunk(x_sample, layers, final_g)
```

```python
import functools

import jax
import jax.numpy as jnp
import numpy as np
from jax import lax
from jax.experimental import pallas as pl
from jax.experimental.pallas import tpu as pltpu

D_MODEL = 1024
D_FF = 2816
GRID_W = 64
EPS = 1e-6
ATT_HEADS = 8
ATT_KV_HEADS = 2
ATT_HEAD_DIM = 64
ROPE_AXIS_DIM = ATT_HEAD_DIM // 2
ROPE_THETA = 10000.0
M_HEADS = 4
M_HEAD_DIM = 128
ATT_WIDTH = ATT_HEADS * ATT_HEAD_DIM
KV_WIDTH = ATT_KV_HEADS * ATT_HEAD_DIM
M_WIDTH = M_HEADS * M_HEAD_DIM
N_GATE_COLS = 4 * M_HEADS
IN_SPLITS = (ATT_WIDTH, KV_WIDTH, KV_WIDTH, M_WIDTH, M_WIDTH, M_WIDTH, N_GATE_COLS, M_WIDTH, D_MODEL, D_MODEL)

V7X_LANES = 128
V7X_SUBLANES = 8
V7X_VMEM_BYTES = 64 * 1024 * 1024

TOKEN_TILE = 512
ATT_Q_TILE = 512
M_CHUNK = 128

ATT_Q_SCALE = ATT_HEAD_DIM ** -0.5 * float(np.log2(np.e))

F32 = jnp.float32
BF16 = jnp.bfloat16
NT_DIMS = (((1,), (1,)), ((), ()))
TN_DIMS = (((0,), (0,)), ((), ()))


def _vmem_limit(nbytes):
    return int(min(nbytes * 1.25 + (4 << 20), V7X_VMEM_BYTES - (6 << 20)))


def _rms_norm(x, g):
    return x * lax.rsqrt(jnp.mean(x * x, axis=-1, keepdims=True) + EPS) * g


def _const_spec(shape):
    return pl.BlockSpec(shape, lambda *_: (0,) * len(shape), pipeline_mode=pl.Buffered(1))


def _ffn_body(x_ref, ng_ref, wg_ref, wu_ref, wd_ref, fin_ref, o_ref, *, apply_final_norm):
    x = x_ref[...]
    xn = _rms_norm(x, ng_ref[...]).astype(BF16)
    g = jnp.dot(xn, wg_ref[...], preferred_element_type=F32)
    u = jnp.dot(xn, wu_ref[...], preferred_element_type=F32)
    a = (g * jax.nn.sigmoid(g) * u).astype(BF16)
    y = x + 0.5 * jnp.dot(a, wd_ref[...], preferred_element_type=F32)
    if apply_final_norm:
        y = _rms_norm(y, fin_ref[...])
    o_ref[...] = y


def _ffn(x2d, norm_g, w_gate, w_up, w_down, final_g, apply_final_norm):
    T = x2d.shape[0]
    tm = TOKEN_TILE
    row = pl.BlockSpec((tm, D_MODEL), lambda i: (i, 0))
    est = (3 * D_MODEL * D_FF * 2 + 4 * tm * D_MODEL * 4 + tm * D_FF * (4 + 4 + 2) + tm * D_MODEL * 8)
    return pl.pallas_call(
        functools.partial(_ffn_body, apply_final_norm=apply_final_norm),
        grid=(T // tm,),
        in_specs=[row, _const_spec((1, D_MODEL)), _const_spec((D_MODEL, D_FF)), _const_spec((D_MODEL, D_FF)),
                  _const_spec((D_FF, D_MODEL)), _const_spec((1, D_MODEL))],
        out_specs=row,
        out_shape=jax.ShapeDtypeStruct((T, D_MODEL), F32),
        compiler_params=pltpu.CompilerParams(dimension_semantics=("parallel",), vmem_limit_bytes=_vmem_limit(est)),
        name="ffn",
    )(x2d, norm_g, w_gate, w_up, w_down, final_g)


def _norm_rope(x, gain, cos, sin_signed, lo, even):
    x2 = x * x
    ss_lo = jnp.sum(jnp.where(lo, x2, 0.0), axis=-1, keepdims=True)
    ss_hi = jnp.sum(jnp.where(lo, 0.0, x2), axis=-1, keepdims=True)
    inv = jnp.where(lo, lax.rsqrt(ss_lo / ATT_HEAD_DIM + EPS), lax.rsqrt(ss_hi / ATT_HEAD_DIM + EPS))
    y = x * inv * gain
    partner = jnp.where(even, pltpu.roll(y, V7X_LANES - 1, 1), pltpu.roll(y, 1, 1))
    return y * cos + partner * sin_signed


def _dup_halves(x, lo):
    xr = pltpu.roll(x, ATT_HEAD_DIM, 1)
    return jnp.where(lo, x, xr), jnp.where(lo, xr, x)


def _proj_body(x_ref, ng_ref, wqkv_ref, wm_ref, wt_ref, wsg_ref, qg_ref, kg_ref, cos_ref, sin_ref, bias_ref,
               q_ref, kd_ref, vd_ref, mqk_ref, mvt_ref, gt_ref, og_ref, ga_ref, gm_ref):
    h = _rms_norm(x_ref[...], ng_ref[...]).astype(BF16)
    tm = h.shape[0]
    lane = lax.broadcasted_iota(jnp.int32, (tm, V7X_LANES), 1)
    lo = lane < ATT_HEAD_DIM
    even = (lane & 1) == 0
    cos = cos_ref[...]
    sin = sin_ref[...]

    qkv = jnp.dot(h, wqkv_ref[...], preferred_element_type=F32)
    for j in range(ATT_WIDTH // V7X_LANES):
        sl = slice(j * V7X_LANES, (j + 1) * V7X_LANES)
        qj = _norm_rope(qkv[:, sl], qg_ref[...], cos, sin, lo, even)
        q_ref[:, sl] = (qj * ATT_Q_SCALE).astype(BF16)
    k = _norm_rope(qkv[:, ATT_WIDTH:ATT_WIDTH + KV_WIDTH], kg_ref[...], cos, sin, lo, even)
    k0, k1 = _dup_halves(k, lo)
    kd_ref[:, :V7X_LANES] = k0.astype(BF16)
    kd_ref[:, V7X_LANES:] = k1.astype(BF16)
    v = qkv[:, ATT_WIDTH + KV_WIDTH:]
    vr = pltpu.roll(v, ATT_HEAD_DIM, 1)
    one_lo = jnp.where(lane == 0, 1.0, 0.0)
    one_hi = jnp.where(lane == ATT_HEAD_DIM, 1.0, 0.0)
    for j, val in enumerate((jnp.where(lo, v, one_hi), jnp.where(lo, one_lo, vr),
                             jnp.where(lo, vr, one_hi), jnp.where(lo, one_lo, v))):
        vd_ref[:, j * V7X_LANES:(j + 1) * V7X_LANES] = val.astype(BF16)

    m = jnp.dot(h, wm_ref[...], preferred_element_type=F32)
    mqk_ref[:, :M_WIDTH] = m[:, :M_WIDTH].astype(BF16)
    mqk_ref[:, M_WIDTH:] = (m[:, M_WIDTH:] * (M_HEAD_DIM ** -0.5)).astype(BF16)

    t = lax.dot_general(wt_ref[...], h, NT_DIMS, preferred_element_type=F32)
    for j in range(tm // M_CHUNK):
        mvt_ref[j] = t[:M_WIDTH, j * M_CHUNK:(j + 1) * M_CHUNK].astype(BF16)
    gt_ref[...] = t[M_WIDTH:, :] + bias_ref[...]

    sg = jax.nn.sigmoid(jnp.dot(h, wsg_ref[...], preferred_element_type=F32))
    og_ref[...] = sg[:, :M_WIDTH].astype(BF16)
    ga_ref[...] = sg[:, M_WIDTH:M_WIDTH + D_MODEL].astype(BF16)
    gm_ref[...] = sg[:, M_WIDTH + D_MODEL:].astype(BF16)


def _proj(x2d, seq_len, norm_g, w_qkv, w_m, w_t, w_sg, q_gain, k_gain, cos_t, sin_t, gate_bias):
    T = x2d.shape[0]
    tm = TOKEN_TILE
    tiles_per_seq = seq_len // tm
    n_qkv, n_m, n_t, n_sg = w_qkv.shape[1], w_m.shape[1], w_t.shape[0], w_sg.shape[1]

    def row(width):
        return pl.BlockSpec((tm, width), lambda i: (i, 0))

    rope = pl.BlockSpec((tm, V7X_LANES), lambda i: (i % tiles_per_seq, 0))
    outs = [
        (jax.ShapeDtypeStruct((T, ATT_WIDTH), BF16), row(ATT_WIDTH)),
        (jax.ShapeDtypeStruct((T, 2 * KV_WIDTH), BF16), row(2 * KV_WIDTH)),
        (jax.ShapeDtypeStruct((T, 4 * KV_WIDTH), BF16), row(4 * KV_WIDTH)),
        (jax.ShapeDtypeStruct((T, 2 * M_WIDTH), BF16), row(2 * M_WIDTH)),
        (jax.ShapeDtypeStruct((T // M_CHUNK, M_WIDTH, M_CHUNK), BF16),
         pl.BlockSpec((tm // M_CHUNK, M_WIDTH, M_CHUNK), lambda i: (i, 0, 0))),
        (jax.ShapeDtypeStruct((N_GATE_COLS, T), F32), pl.BlockSpec((N_GATE_COLS, tm), lambda i: (0, i))),
        (jax.ShapeDtypeStruct((T, M_WIDTH), BF16), row(M_WIDTH)),
        (jax.ShapeDtypeStruct((T, D_MODEL), BF16), row(D_MODEL)),
        (jax.ShapeDtypeStruct((T, D_MODEL), BF16), row(D_MODEL)),
    ]
    n_out_bf16 = ATT_WIDTH + 4 * KV_WIDTH + 4 * M_WIDTH + 2 * D_MODEL
    est = (D_MODEL * (n_qkv + n_m + n_t + n_sg) * 2 + 2 * tm * D_MODEL * 4
           + tm * (n_qkv + n_m + n_t + n_sg) * 6 + 2 * tm * n_out_bf16 * 2 + 8 * tm * V7X_LANES * 4)
    return pl.pallas_call(
        _proj_body,
        grid=(T // tm,),
        in_specs=[row(D_MODEL), _const_spec((1, D_MODEL)), _const_spec((D_MODEL, n_qkv)), _const_spec((D_MODEL, n_m)),
                  _const_spec((n_t, D_MODEL)), _const_spec((D_MODEL, n_sg)),
                  _const_spec((1, V7X_LANES)), _const_spec((1, V7X_LANES)), rope, rope,
                  _const_spec((N_GATE_COLS, tm))],
        out_specs=[s for _, s in outs],
        out_shape=[o for o, _ in outs],
        compiler_params=pltpu.CompilerParams(dimension_semantics=("parallel",), vmem_limit_bytes=_vmem_limit(est)),
        name="proj",
    )(x2d, norm_g, w_qkv, w_m, w_t, w_sg, q_gain, k_gain, cos_t, sin_t, gate_bias)


def _attn_body(q_ref, kd_ref, vd_ref, o_ref):
    tq = q_ref.shape[1]
    lane = lax.broadcasted_iota(jnp.int32, (tq, V7X_LANES), 1)
    lo = lane < ATT_HEAD_DIM
    pairs_per_kv = (ATT_HEADS // ATT_KV_HEADS) // 2
    for g in range(ATT_KV_HEADS):
        kd = kd_ref[0, :, g * V7X_LANES:(g + 1) * V7X_LANES]
        for p in range(pairs_per_kv):
            c0 = (g * pairs_per_kv + p) * V7X_LANES
            q2 = q_ref[0, :, c0:c0 + V7X_LANES]
            outs = []
            for half, keep in enumerate((lo, jnp.logical_not(lo))):
                qm = jnp.where(keep, q2, jnp.zeros_like(q2))
                s = lax.dot_general(qm, kd, NT_DIMS, preferred_element_type=F32)
                e = jnp.exp2(s - jnp.max(s, axis=-1, keepdims=True))
                vd = vd_ref[0, :, (2 * g + half) * V7X_LANES:(2 * g + half + 1) * V7X_LANES]
                o = jnp.dot(e.astype(BF16), vd, preferred_element_type=F32)
                d0 = (1 - half) * ATT_HEAD_DIM
                outs.append(o * (1.0 / o[:, d0:d0 + 1]))
            o_ref[0, :, c0:c0 + V7X_LANES] = jnp.where(lo, outs[0], outs[1]).astype(BF16)


def _attention(q, kd, vd):
    B, S, _ = q.shape
    tq = ATT_Q_TILE
    est = (2 * tq * ATT_WIDTH * 2 * 2 + 2 * S * 6 * KV_WIDTH * 2 + 3 * tq * S * 4)
    return pl.pallas_call(
        _attn_body,
        grid=(B, S // tq),
        in_specs=[pl.BlockSpec((1, tq, ATT_WIDTH), lambda b, i: (b, i, 0)),
                  pl.BlockSpec((1, S, 2 * KV_WIDTH), lambda b, i: (b, 0, 0)),
                  pl.BlockSpec((1, S, 4 * KV_WIDTH), lambda b, i: (b, 0, 0))],
        out_specs=pl.BlockSpec((1, tq, ATT_WIDTH), lambda b, i: (b, i, 0)),
        out_shape=jax.ShapeDtypeStruct((B, S, ATT_WIDTH), BF16),
        compiler_params=pltpu.CompilerParams(dimension_semantics=("parallel", "parallel"),
                                             vmem_limit_bytes=_vmem_limit(est)),
        name="attn",
    )(q, kd, vd)


def _split3(x):
    hi = x.astype(BF16)
    r = x - hi.astype(F32)
    mid = r.astype(BF16)
    lo = (r - mid.astype(F32)).astype(BF16)
    return hi, mid, lo


def _log_sigmoid(x):
    return jnp.minimum(x, 0.0) - jnp.log1p(jnp.exp(-jnp.abs(x)))


M_AUG = M_HEAD_DIM + 16


def _mlstm_body(q_ref, k_ref, vt_ref, g_ref, og_ref, hg_ref, o_ref,
                b_ref, x_ref, w_ref, bl_ref, gx_ref, ht_ref):
    L = M_CHUNK
    nc = vt_ref.shape[0]
    hd = pl.program_id(1)
    src = lax.broadcasted_iota(jnp.int32, (L, L), 0)
    dst = lax.broadcasted_iota(jnp.int32, (L, L), 1)
    visible = (src <= dst, src >= dst)

    for d in range(2):
        fwd = d == 0
        li = g_ref[2 * d * M_HEADS + hd]
        lf = _log_sigmoid(g_ref[(2 * d + 1) * M_HEADS + hd])
        tri = jnp.where(visible[d], 1.0, 0.0).astype(BF16)
        hi, mid, lo3 = _split3(lf)
        b = (jnp.dot(hi, tri, preferred_element_type=F32) + jnp.dot(mid, tri, preferred_element_type=F32)
             + jnp.dot(lo3, tri, preferred_element_type=F32))
        b_end = b[:, L - 1:L] if fwd else b[:, 0:1]
        g = b_end - b + li
        gmax = jnp.max(g, axis=1, keepdims=True)
        b_ref[d] = b
        x_ref[d] = li - b
        w_ref[d] = jnp.exp(g - gmax)
        bl_ref[d] = jnp.broadcast_to(b_end, (nc, L))
        gx_ref[d] = jnp.broadcast_to(gmax, (nc, L))

    ones_rows = jnp.where(lax.broadcasted_iota(jnp.int32, (M_AUG - M_HEAD_DIM, L), 0) == 0, 1.0, 0.0).astype(BF16)

    def step(i, carry):
        out = []
        for d in range(2):
            ct, m = carry[d]
            c = i if d == 0 else nc - 1 - i
            r0 = pl.multiple_of(c * L, L)
            qc = q_ref[0, pl.ds(r0, L), :]
            kc = k_ref[0, pl.ds(r0, L), :]
            vta = jnp.concatenate([vt_ref[c], ones_rows], axis=0)
            brow = b_ref[d, pl.ds(c, 1), :]
            xrow = x_ref[d, pl.ds(c, 1), :]
            wrow = w_ref[d, pl.ds(c, 1), :]
            b_end = bl_ref[d, pl.ds(c, 1), :]
            gmax = gx_ref[d, pl.ds(c, 1), :]

            ut = jnp.dot((vta.astype(F32) * wrow).astype(BF16), kc, preferred_element_type=F32)

            both = lax.dot_general(jnp.concatenate([kc, ct.astype(BF16)], axis=0), qc, NT_DIMS,
                                   preferred_element_type=F32)
            xcol = jnp.broadcast_to(xrow, (L, L)).T
            dmat = jnp.where(visible[d], xcol + brow, -jnp.inf)
            a = jnp.max(dmat, axis=0, keepdims=True)
            pt = both[:L] * jnp.exp(dmat - a)
            intra = jnp.dot(vta, pt.astype(BF16), preferred_element_type=F32)
            m_in = brow + m
            m_t = jnp.maximum(a, m_in)
            comb = jnp.exp(a - m_t) * intra + jnp.exp(m_in - m_t) * both[L:]
            den = comb[M_HEAD_DIM:M_HEAD_DIM + 1]
            ht_ref[d, c] = comb[:M_HEAD_DIM] * (1.0 / jnp.maximum(jnp.abs(den), jnp.exp(-m_t)))

            m_new = jnp.maximum(b_end + m, gmax)
            out.append((jnp.exp(b_end + m - m_new) * ct + jnp.exp(gmax - m_new) * ut, m_new))
        return tuple(out)

    init = (jnp.zeros((M_AUG, M_HEAD_DIM), F32), jnp.zeros((1, L), F32))
    lax.fori_loop(0, nc, step, (init, init), unroll=2)

    gain = hg_ref[...]

    def finish(c, carry):
        r0 = pl.multiple_of(c * L, L)
        ht = ht_ref[0, c] + ht_ref[1, c]
        inv = lax.rsqrt(jnp.mean(ht * ht, axis=0, keepdims=True) + EPS)
        hn = (ht * inv * gain).T
        o_ref[0, pl.ds(r0, L), :] = (og_ref[0, pl.ds(r0, L), :].astype(F32) * hn).astype(BF16)
        return carry

    lax.fori_loop(0, nc, finish, 0, unroll=8)


def _mlstm(mqk, mvt, gates_t, og, head_gain):
    B, S, _ = mqk.shape
    L = M_CHUNK
    nc = S // L
    d = M_HEAD_DIM

    def head_block(offset):
        return pl.BlockSpec((1, S, d), lambda b, h: (b, 0, offset + h))

    scratch = [pltpu.VMEM((2, nc, L), F32) for _ in range(5)] + [pltpu.VMEM((2, nc, d, L), F32)]
    est = 2 * (2 * S * d * 2 + nc * d * L * 2 + N_GATE_COLS * nc * L * 4 + 2 * S * d * 2) + 3 * nc * d * L * 4
    return pl.pallas_call(
        _mlstm_body,
        grid=(B, M_HEADS),
        in_specs=[head_block(0), head_block(M_HEADS),
                  pl.BlockSpec((nc, d, L), lambda b, h: (b, h, 0)),
                  pl.BlockSpec((N_GATE_COLS, nc, L), lambda b, h: (0, b, 0)),
                  head_block(0),
                  pl.BlockSpec((d, V7X_LANES), lambda b, h: (h, 0))],
        out_specs=head_block(0),
        out_shape=jax.ShapeDtypeStruct((B, S, M_WIDTH), BF16),
        scratch_shapes=scratch,
        compiler_params=pltpu.CompilerParams(dimension_semantics=("parallel", "arbitrary"),
                                             vmem_limit_bytes=_vmem_limit(est)),
        name="mlstm",
    )(mqk, mqk, mvt, gates_t, og, head_gain)


def _merge_body(x_ref, att_ref, mem_ref, ga_ref, gm_ref, wa_ref, wm_ref, wo_ref, o_ref):
    ya = jnp.dot(att_ref[...], wa_ref[...], preferred_element_type=F32)
    ym = jnp.dot(mem_ref[...], wm_ref[...], preferred_element_type=F32)
    merged = (ga_ref[...].astype(F32) * ya + gm_ref[...].astype(F32) * ym).astype(BF16)
    o_ref[...] = x_ref[...] + jnp.dot(merged, wo_ref[...], preferred_element_type=F32)


def _merge(x2d, att, mem, ga, gm, w_ua, w_um, w_out):
    T = x2d.shape[0]
    tm = TOKEN_TILE

    def row(width):
        return pl.BlockSpec((tm, width), lambda i: (i, 0))

    est = ((ATT_WIDTH + M_WIDTH + D_MODEL) * D_MODEL * 2 + 4 * tm * D_MODEL * 4
           + 2 * tm * (ATT_WIDTH + M_WIDTH + 2 * D_MODEL) * 2 + 4 * tm * D_MODEL * 4)
    return pl.pallas_call(
        _merge_body,
        grid=(T // tm,),
        in_specs=[row(D_MODEL), row(ATT_WIDTH), row(M_WIDTH), row(D_MODEL), row(D_MODEL),
                  _const_spec((ATT_WIDTH, D_MODEL)), _const_spec((M_WIDTH, D_MODEL)), _const_spec((D_MODEL, D_MODEL))],
        out_specs=row(D_MODEL),
        out_shape=jax.ShapeDtypeStruct((T, D_MODEL), F32),
        compiler_params=pltpu.CompilerParams(dimension_semantics=("parallel",), vmem_limit_bytes=_vmem_limit(est)),
        name="merge",
    )(x2d, att, mem, ga, gm, w_ua, w_um, w_out)


def _rope_tables(seq_len):
    t = np.arange(seq_len)
    inv_freq = ROPE_THETA ** (-np.arange(0, ROPE_AXIS_DIM, 2, dtype=np.float32) / ROPE_AXIS_DIM)
    ang = np.concatenate([(t // GRID_W).astype(np.float32)[:, None] * inv_freq,
                          (t % GRID_W).astype(np.float32)[:, None] * inv_freq], axis=-1).astype(np.float32)
    ang = jnp.asarray(ang)
    cos = jnp.repeat(jnp.cos(ang), 2, axis=-1)
    sin = jnp.repeat(jnp.sin(ang), 2, axis=-1) * jnp.tile(jnp.asarray([-1.0, 1.0], F32), ATT_HEAD_DIM // 2)
    return jnp.tile(cos, (1, 2)), jnp.tile(sin, (1, 2))


def _layer_params(l, ffn1_norm, ffn1_w_gate, ffn1_w_up, ffn1_w_down, mix_norm, w_in, q_norm, k_norm,
                  mlstm_gate_bias, mlstm_head_norm, w_up_att, w_up_mlstm, w_out,
                  ffn2_norm, ffn2_w_gate, ffn2_w_up, ffn2_w_down):
    offs = np.concatenate([[0], np.cumsum(IN_SPLITS)])
    w = w_in[l].astype(BF16)
    seg = [w[:, offs[i]:offs[i + 1]] for i in range(len(IN_SPLITS))]
    return dict(
        ffn1=(ffn1_norm[l][None], ffn1_w_gate[l].astype(BF16), ffn1_w_up[l].astype(BF16), ffn1_w_down[l].astype(BF16)),
        ffn2=(ffn2_norm[l][None], ffn2_w_gate[l].astype(BF16), ffn2_w_up[l].astype(BF16), ffn2_w_down[l].astype(BF16)),
        mix_norm=mix_norm[l][None],
        w_qkv=jnp.concatenate(seg[0:3], axis=1),
        w_m=jnp.concatenate(seg[3:5], axis=1),
        w_t=jnp.concatenate([seg[5].T, seg[6].T], axis=0),
        w_sg=jnp.concatenate(seg[7:10], axis=1),
        q_gain=jnp.tile(q_norm[l], 2)[None],
        k_gain=jnp.tile(k_norm[l], 2)[None],
        gate_bias=jnp.broadcast_to(mlstm_gate_bias[l][:, None], (N_GATE_COLS, TOKEN_TILE)),
        head_gain=jnp.broadcast_to(mlstm_head_norm[l][:, None], (M_WIDTH, V7X_LANES)),
        w_ua=w_up_att[l].astype(BF16), w_um=w_up_mlstm[l].astype(BF16), w_out=w_out[l].astype(BF16),
    )


def _trunk(x, layers, final_g):
    B, S, D = x.shape
    cos_t, sin_t = _rope_tables(S)
    x2d = x.reshape(B * S, D)
    for i, p in enumerate(layers):
        last = i == len(layers) - 1
        x1 = _ffn(x2d, *p["ffn1"], final_g, False)
        q, kd, vd, mqk, mvt, gates_t, og, ga, gm = _proj(x1, S, p["mix_norm"], p["w_qkv"], p["w_m"], p["w_t"], p["w_sg"],
                                                         p["q_gain"], p["k_gain"], cos_t, sin_t, p["gate_bias"])
        att = _attention(q.reshape(B, S, -1), kd.reshape(B, S, -1), vd.reshape(B, S, -1))
        mem = _mlstm(mqk.reshape(B, S, -1), mvt, gates_t.reshape(N_GATE_COLS, -1, M_CHUNK), og.reshape(B, S, -1),
                     p["head_gain"])
        x2 = _merge(x1, att.reshape(B * S, -1), mem.reshape(B * S, -1), ga, gm, p["w_ua"], p["w_um"], p["w_out"])
        x2d = _ffn(x2, *p["ffn2"], final_g, last)
        if last:
            return x2d.reshape(B, S, D)
    return _rms_norm(x2d, final_g).reshape(B, S, D)


def kernel(x_prompt, x_sample, ffn1_norm, ffn1_w_gate, ffn1_w_up, ffn1_w_down, mix_norm, w_in, q_norm, k_norm,
           mlstm_gate_bias, mlstm_head_norm, w_up_att, w_up_mlstm, w_out,
           ffn2_norm, ffn2_w_gate, ffn2_w_up, ffn2_w_down, final_norm):
    depth = w_in.shape[0]
    layers = [_layer_params(l, ffn1_norm, ffn1_w_gate, ffn1_w_up, ffn1_w_down, mix_norm, w_in, q_norm, k_norm,
                            mlstm_gate_bias, mlstm_head_norm, w_up_att, w_up_mlstm, w_out,
                            ffn2_norm, ffn2_w_gate, ffn2_w_up, ffn2_w_down) for l in range(depth)]
    final_g = final_norm[None]
    return _trunk(x_prompt, layers, final_g), _trunk(x_sample, layers, final_g)
```

```python
import functools

import jax
import jax.numpy as jnp
import numpy as np
from jax import lax
from jax.experimental import pallas as pl
from jax.experimental.pallas import tpu as pltpu

D_MODEL = 1024
D_FF = 2816
GRID_W = 64
EPS = 1e-6
ATT_HEADS = 8
ATT_KV_HEADS = 2
ATT_HEAD_DIM = 64
ROPE_AXIS_DIM = ATT_HEAD_DIM // 2
ROPE_THETA = 10000.0
M_HEADS = 4
M_HEAD_DIM = 128
ATT_WIDTH = ATT_HEADS * ATT_HEAD_DIM
KV_WIDTH = ATT_KV_HEADS * ATT_HEAD_DIM
M_WIDTH = M_HEADS * M_HEAD_DIM
N_GATE_COLS = 4 * M_HEADS
IN_SPLITS = (ATT_WIDTH, KV_WIDTH, KV_WIDTH, M_WIDTH, M_WIDTH, M_WIDTH, N_GATE_COLS, M_WIDTH, D_MODEL, D_MODEL)

V7X_LANES = 128
V7X_SUBLANES = 8
V7X_VMEM_BYTES = 64 * 1024 * 1024

TOKEN_TILE = 512
ATT_Q_TILE = 512
M_CHUNK = 128
M_HEADS_PER_STEP = 4
M_STEP_UNROLL = 1

ATT_Q_SCALE = ATT_HEAD_DIM ** -0.5 * float(np.log2(np.e))

F32 = jnp.float32
BF16 = jnp.bfloat16
NT_DIMS = (((1,), (1,)), ((), ()))


def _vmem_limit(nbytes):
    return int(min(nbytes * 1.25 + (4 << 20), V7X_VMEM_BYTES - (6 << 20)))


def _rms_norm(x, g):
    return x * lax.rsqrt(jnp.mean(x * x, axis=-1, keepdims=True) + EPS) * g


def _const_spec(shape):
    return pl.BlockSpec(shape, lambda *_: (0,) * len(shape), pipeline_mode=pl.Buffered(1))


def _swiglu_half_step(x, ng_ref, wg_ref, wu_ref, wd_ref):
    xn = _rms_norm(x, ng_ref[...]).astype(BF16)
    g = jnp.dot(xn, wg_ref[...], preferred_element_type=F32)
    u = jnp.dot(xn, wu_ref[...], preferred_element_type=F32)
    a = (g * jax.nn.sigmoid(g) * u).astype(BF16)
    return x + 0.5 * jnp.dot(a, wd_ref[...], preferred_element_type=F32)


def _ffn_vmem_bytes(tm):
    return 3 * D_MODEL * D_FF * 2 + 4 * tm * D_MODEL * 4 + tm * D_FF * (4 + 4 + 2) + tm * D_MODEL * 8


def _ffn_weight_specs():
    return [_const_spec((1, D_MODEL)), _const_spec((D_MODEL, D_FF)), _const_spec((D_MODEL, D_FF)),
            _const_spec((D_FF, D_MODEL))]


def _ffn_body(x_ref, ng_ref, wg_ref, wu_ref, wd_ref, o_ref):
    o_ref[...] = _swiglu_half_step(x_ref[...], ng_ref, wg_ref, wu_ref, wd_ref)


def _ffn(x2d, norm_g, w_gate, w_up, w_down):
    T = x2d.shape[0]
    tm = TOKEN_TILE
    row = pl.BlockSpec((tm, D_MODEL), lambda i: (i, 0))
    return pl.pallas_call(
        _ffn_body,
        grid=(T // tm,),
        in_specs=[row] + _ffn_weight_specs(),
        out_specs=row,
        out_shape=jax.ShapeDtypeStruct((T, D_MODEL), F32),
        compiler_params=pltpu.CompilerParams(dimension_semantics=("parallel",),
                                             vmem_limit_bytes=_vmem_limit(_ffn_vmem_bytes(tm))),
        name="ffn",
    )(x2d, norm_g, w_gate, w_up, w_down)


def _norm_rope(x, gain, cos, sin_signed, lo, even):
    x2 = x * x
    ss_lo = jnp.sum(jnp.where(lo, x2, 0.0), axis=-1, keepdims=True)
    ss_hi = jnp.sum(jnp.where(lo, 0.0, x2), axis=-1, keepdims=True)
    inv = jnp.where(lo, lax.rsqrt(ss_lo / ATT_HEAD_DIM + EPS), lax.rsqrt(ss_hi / ATT_HEAD_DIM + EPS))
    y = x * inv * gain
    partner = jnp.where(even, pltpu.roll(y, V7X_LANES - 1, 1), pltpu.roll(y, 1, 1))
    return y * cos + partner * sin_signed


def _dup_halves(x, lo):
    xr = pltpu.roll(x, ATT_HEAD_DIM, 1)
    return jnp.where(lo, x, xr), jnp.where(lo, xr, x)


def _proj_body(x_ref, ng_ref, wqkv_ref, wm_ref, wt_ref, wsg_ref, qg_ref, kg_ref, cos_ref, sin_ref, bias_ref,
               q_ref, kd_ref, vd_ref, mqk_ref, mvt_ref, gt_ref, og_ref, ga_ref, gm_ref):
    h = _rms_norm(x_ref[...], ng_ref[...]).astype(BF16)
    tm = h.shape[0]
    lane = lax.broadcasted_iota(jnp.int32, (tm, V7X_LANES), 1)
    lo = lane < ATT_HEAD_DIM
    even = (lane & 1) == 0
    cos = cos_ref[...]
    sin = sin_ref[...]

    qkv = jnp.dot(h, wqkv_ref[...], preferred_element_type=F32)
    for j in range(ATT_WIDTH // V7X_LANES):
        sl = slice(j * V7X_LANES, (j + 1) * V7X_LANES)
        qj = _norm_rope(qkv[:, sl], qg_ref[...], cos, sin, lo, even)
        q_ref[:, sl] = (qj * ATT_Q_SCALE).astype(BF16)
    k = _norm_rope(qkv[:, ATT_WIDTH:ATT_WIDTH + KV_WIDTH], kg_ref[...], cos, sin, lo, even)
    k0, k1 = _dup_halves(k, lo)
    kd_ref[:, :V7X_LANES] = k0.astype(BF16)
    kd_ref[:, V7X_LANES:] = k1.astype(BF16)
    v = qkv[:, ATT_WIDTH + KV_WIDTH:]
    vr = pltpu.roll(v, ATT_HEAD_DIM, 1)
    one_lo = jnp.where(lane == 0, 1.0, 0.0)
    one_hi = jnp.where(lane == ATT_HEAD_DIM, 1.0, 0.0)
    for j, val in enumerate((jnp.where(lo, v, one_hi), jnp.where(lo, one_lo, vr),
                             jnp.where(lo, vr, one_hi), jnp.where(lo, one_lo, v))):
        vd_ref[:, j * V7X_LANES:(j + 1) * V7X_LANES] = val.astype(BF16)

    m = jnp.dot(h, wm_ref[...], preferred_element_type=F32)
    mqk_ref[:, :M_WIDTH] = m[:, :M_WIDTH].astype(BF16)
    mqk_ref[:, M_WIDTH:] = (m[:, M_WIDTH:] * (M_HEAD_DIM ** -0.5)).astype(BF16)

    t = lax.dot_general(wt_ref[...], h, NT_DIMS, preferred_element_type=F32)
    for j in range(tm // M_CHUNK):
        mvt_ref[j] = t[:M_WIDTH, j * M_CHUNK:(j + 1) * M_CHUNK].astype(BF16)
    gt_ref[...] = t[M_WIDTH:, :] + bias_ref[...]

    sg = jax.nn.sigmoid(jnp.dot(h, wsg_ref[...], preferred_element_type=F32))
    og_ref[...] = sg[:, :M_WIDTH].astype(BF16)
    ga_ref[...] = sg[:, M_WIDTH:M_WIDTH + D_MODEL].astype(BF16)
    gm_ref[...] = sg[:, M_WIDTH + D_MODEL:].astype(BF16)


def _proj(x2d, seq_len, norm_g, w_qkv, w_m, w_t, w_sg, q_gain, k_gain, cos_t, sin_t, gate_bias):
    T = x2d.shape[0]
    tm = TOKEN_TILE
    tiles_per_seq = seq_len // tm
    n_qkv, n_m, n_t, n_sg = w_qkv.shape[1], w_m.shape[1], w_t.shape[0], w_sg.shape[1]

    def row(width):
        return pl.BlockSpec((tm, width), lambda i: (i, 0))

    rope = pl.BlockSpec((tm, V7X_LANES), lambda i: (i % tiles_per_seq, 0))
    outs = [
        (jax.ShapeDtypeStruct((T, ATT_WIDTH), BF16), row(ATT_WIDTH)),
        (jax.ShapeDtypeStruct((T, 2 * KV_WIDTH), BF16), row(2 * KV_WIDTH)),
        (jax.ShapeDtypeStruct((T, 4 * KV_WIDTH), BF16), row(4 * KV_WIDTH)),
        (jax.ShapeDtypeStruct((T, 2 * M_WIDTH), BF16), row(2 * M_WIDTH)),
        (jax.ShapeDtypeStruct((T // M_CHUNK, M_WIDTH, M_CHUNK), BF16),
         pl.BlockSpec((tm // M_CHUNK, M_WIDTH, M_CHUNK), lambda i: (i, 0, 0))),
        (jax.ShapeDtypeStruct((N_GATE_COLS, T), F32), pl.BlockSpec((N_GATE_COLS, tm), lambda i: (0, i))),
        (jax.ShapeDtypeStruct((T, M_WIDTH), BF16), row(M_WIDTH)),
        (jax.ShapeDtypeStruct((T, D_MODEL), BF16), row(D_MODEL)),
        (jax.ShapeDtypeStruct((T, D_MODEL), BF16), row(D_MODEL)),
    ]
    n_out_bf16 = ATT_WIDTH + 4 * KV_WIDTH + 4 * M_WIDTH + 2 * D_MODEL
    est = (D_MODEL * (n_qkv + n_m + n_t + n_sg) * 2 + 2 * tm * D_MODEL * 4
           + tm * (n_qkv + n_m + n_t + n_sg) * 6 + 2 * tm * n_out_bf16 * 2 + 8 * tm * V7X_LANES * 4)
    return pl.pallas_call(
        _proj_body,
        grid=(T // tm,),
        in_specs=[row(D_MODEL), _const_spec((1, D_MODEL)), _const_spec((D_MODEL, n_qkv)), _const_spec((D_MODEL, n_m)),
                  _const_spec((n_t, D_MODEL)), _const_spec((D_MODEL, n_sg)),
                  _const_spec((1, V7X_LANES)), _const_spec((1, V7X_LANES)), rope, rope,
                  _const_spec((N_GATE_COLS, tm))],
        out_specs=[s for _, s in outs],
        out_shape=[o for o, _ in outs],
        compiler_params=pltpu.CompilerParams(dimension_semantics=("parallel",), vmem_limit_bytes=_vmem_limit(est)),
        name="proj",
    )(x2d, norm_g, w_qkv, w_m, w_t, w_sg, q_gain, k_gain, cos_t, sin_t, gate_bias)


def _attn_body(q_ref, kd_ref, vd_ref, o_ref):
    tq = q_ref.shape[1]
    lane = lax.broadcasted_iota(jnp.int32, (tq, V7X_LANES), 1)
    lo = lane < ATT_HEAD_DIM
    pairs_per_kv = (ATT_HEADS // ATT_KV_HEADS) // 2
    for g in range(ATT_KV_HEADS):
        kd = kd_ref[0, :, g * V7X_LANES:(g + 1) * V7X_LANES]
        for p in range(pairs_per_kv):
            c0 = (g * pairs_per_kv + p) * V7X_LANES
            q2 = q_ref[0, :, c0:c0 + V7X_LANES]
            outs = []
            for half, keep in enumerate((lo, jnp.logical_not(lo))):
                qm = jnp.where(keep, q2, jnp.zeros_like(q2))
                s = lax.dot_general(qm, kd, NT_DIMS, preferred_element_type=F32)
                e = jnp.exp2(s - jnp.max(s, axis=-1, keepdims=True))
                vd = vd_ref[0, :, (2 * g + half) * V7X_LANES:(2 * g + half + 1) * V7X_LANES]
                o = jnp.dot(e.astype(BF16), vd, preferred_element_type=F32)
                d0 = (1 - half) * ATT_HEAD_DIM
                outs.append(o * (1.0 / o[:, d0:d0 + 1]))
            o_ref[0, :, c0:c0 + V7X_LANES] = jnp.where(lo, outs[0], outs[1]).astype(BF16)


def _attention(q, kd, vd):
    B, S, _ = q.shape
    tq = ATT_Q_TILE
    est = (2 * tq * ATT_WIDTH * 2 * 2 + 2 * S * 6 * KV_WIDTH * 2 + 3 * tq * S * 4)
    return pl.pallas_call(
        _attn_body,
        grid=(B, S // tq),
        in_specs=[pl.BlockSpec((1, tq, ATT_WIDTH), lambda b, i: (b, i, 0)),
                  pl.BlockSpec((1, S, 2 * KV_WIDTH), lambda b, i: (b, 0, 0)),
                  pl.BlockSpec((1, S, 4 * KV_WIDTH), lambda b, i: (b, 0, 0))],
        out_specs=pl.BlockSpec((1, tq, ATT_WIDTH), lambda b, i: (b, i, 0)),
        out_shape=jax.ShapeDtypeStruct((B, S, ATT_WIDTH), BF16),
        compiler_params=pltpu.CompilerParams(dimension_semantics=("parallel", "parallel"),
                                             vmem_limit_bytes=_vmem_limit(est)),
        name="attn",
    )(q, kd, vd)


def _split3(x):
    hi = x.astype(BF16)
    r = x - hi.astype(F32)
    mid = r.astype(BF16)
    lo = (r - mid.astype(F32)).astype(BF16)
    return hi, mid, lo


def _log_sigmoid(x):
    return jnp.minimum(x, 0.0) - jnp.log1p(jnp.exp(-jnp.abs(x)))


M_AUG = M_HEAD_DIM + 16


def _mlstm_body(q_ref, k_ref, vt_ref, g_ref, og_ref, hg_ref, o_ref,
                b_ref, x_ref, w_ref, bl_ref, gx_ref, ht_ref):
    L = M_CHUNK
    D = M_HEAD_DIM
    nc = vt_ref.shape[0]
    heads = q_ref.shape[2] // D
    head0 = pl.program_id(1) * heads
    streams = [(h, d) for h in range(heads) for d in range(2)]
    src = lax.broadcasted_iota(jnp.int32, (L, L), 0)
    dst = lax.broadcasted_iota(jnp.int32, (L, L), 1)
    visible = (src <= dst, src >= dst)
    tri = [jnp.where(v, 1.0, 0.0).astype(BF16) for v in visible]

    for s, (h, d) in enumerate(streams):
        li = g_ref[2 * d * M_HEADS + head0 + h]
        lf = _log_sigmoid(g_ref[(2 * d + 1) * M_HEADS + head0 + h])
        hi, mid, lo3 = _split3(lf)
        b = (jnp.dot(hi, tri[d], preferred_element_type=F32) + jnp.dot(mid, tri[d], preferred_element_type=F32)
             + jnp.dot(lo3, tri[d], preferred_element_type=F32))
        b_end = b[:, L - 1:L] if d == 0 else b[:, 0:1]
        g = b_end - b + li
        gmax = jnp.max(g, axis=1, keepdims=True)
        b_ref[s] = b
        x_ref[s] = li - b
        w_ref[s] = jnp.exp(g - gmax)
        bl_ref[s] = jnp.broadcast_to(b_end, (nc, L))
        gx_ref[s] = jnp.broadcast_to(gmax, (nc, L))

    ones_rows = jnp.where(lax.broadcasted_iota(jnp.int32, (M_AUG - D, L), 0) == 0, 1.0, 0.0).astype(BF16)

    def step(i, carry, *, complete):
        states, pending = carry
        for s in range(len(streams)):
            emit(i - 1, s, pending[s])
        if complete:
            write_out(i - 1)
            write_out(nc - i)
        nxt = [advance(i, s, states[s]) for s in range(len(streams))]
        return tuple(n[0] for n in nxt), tuple(n[1] for n in nxt)

    def chunk_of(i, d):
        return i if d == 0 else nc - 1 - i

    def value_rows(c, h):
        return jnp.concatenate([vt_ref[c, h * D:(h + 1) * D, :], ones_rows], axis=0)

    def advance(i, s, state):
        h, d = streams[s]
        ct, m = state
        c = chunk_of(i, d)
        r0 = c * L if isinstance(c, int) else pl.multiple_of(c * L, L)
        qc = q_ref[0, pl.ds(r0, L), h * D:(h + 1) * D]
        kc = k_ref[0, pl.ds(r0, L), h * D:(h + 1) * D]
        brow = b_ref[s, pl.ds(c, 1), :]
        xrow = x_ref[s, pl.ds(c, 1), :]
        wrow = w_ref[s, pl.ds(c, 1), :]
        b_end = bl_ref[s, pl.ds(c, 1), :]
        gmax = gx_ref[s, pl.ds(c, 1), :]

        ut = jnp.dot((value_rows(c, h).astype(F32) * wrow).astype(BF16), kc, preferred_element_type=F32)

        both = lax.dot_general(jnp.concatenate([kc, ct.astype(BF16)], axis=0), qc, NT_DIMS,
                               preferred_element_type=F32)
        xcol = jnp.broadcast_to(xrow, (L, L)).T
        dmat = jnp.where(visible[d], xcol + brow, -jnp.inf)
        a = jnp.max(dmat, axis=0, keepdims=True)
        pt = (both[:L] * jnp.exp(dmat - a)).astype(BF16)
        m_in = brow + m
        m_t = jnp.maximum(a, m_in)
        pend = (pt, both[L:], jnp.exp(a - m_t), jnp.exp(m_in - m_t), jnp.exp(-m_t))

        m_new = jnp.maximum(b_end + m, gmax)
        keep = jnp.exp(b_end + m - m_new)[:, :D]
        add = jnp.exp(gmax - m_new)[:, :D]
        return (keep * ct + add * ut, m_new), pend

    def emit(i, s, pend):
        h, d = streams[s]
        c = chunk_of(i, d)
        pt, inter, wi, wo, floor = pend
        intra = jnp.dot(value_rows(c, h), pt, preferred_element_type=F32)
        comb = wi * intra + wo * inter
        ht_ref[s, c] = comb[:D] * (1.0 / jnp.maximum(jnp.abs(comb[D:D + 1]), floor))

    def write_out(c):
        r0 = c * L if isinstance(c, int) else pl.multiple_of(c * L, L)
        for h in range(heads):
            ht = ht_ref[2 * h, c] + ht_ref[2 * h + 1, c]
            inv = lax.rsqrt(jnp.mean(ht * ht, axis=0, keepdims=True) + EPS)
            hn = (ht * inv * hg_ref[h * D:(h + 1) * D, 0:1]).T
            og = og_ref[0, pl.ds(r0, L), h * D:(h + 1) * D]
            o_ref[0, pl.ds(r0, L), h * D:(h + 1) * D] = (og.astype(F32) * hn).astype(BF16)

    assert nc % 2 == 0
    meet = nc // 2 + 1
    init = (jnp.zeros((M_AUG, D), F32), jnp.zeros((1, L), F32))
    first = [advance(0, s, init) for s in range(len(streams))]
    carry = (tuple(f[0] for f in first), tuple(f[1] for f in first))
    carry = lax.fori_loop(1, meet, functools.partial(step, complete=False), carry, unroll=M_STEP_UNROLL)
    _, pending = lax.fori_loop(meet, nc, functools.partial(step, complete=True), carry, unroll=M_STEP_UNROLL)
    for s in range(len(streams)):
        emit(nc - 1, s, pending[s])
    write_out(nc - 1)
    write_out(0)


def _mlstm(mqk, mvt, gates_t, og, head_gain):
    B, S, _ = mqk.shape
    L = M_CHUNK
    nc = S // L
    hb = M_HEADS_PER_STEP
    w = hb * M_HEAD_DIM
    groups = M_HEADS // hb

    def head_block(offset):
        return pl.BlockSpec((1, S, w), lambda b, h: (b, 0, offset + h))

    scratch = [pltpu.VMEM((2 * hb, nc, L), F32) for _ in range(5)] + [pltpu.VMEM((2 * hb, nc, M_HEAD_DIM, L), F32)]
    est = (2 * (2 * S * w * 2 + nc * w * L * 2 + N_GATE_COLS * nc * L * 4 + 2 * S * w * 2)
           + 2 * hb * nc * M_HEAD_DIM * L * 4 + 16 * M_AUG * L * 4 * 2 * hb)
    return pl.pallas_call(
        _mlstm_body,
        grid=(B, groups),
        in_specs=[head_block(0), head_block(groups),
                  pl.BlockSpec((nc, w, L), lambda b, h: (b, h, 0)),
                  pl.BlockSpec((N_GATE_COLS, nc, L), lambda b, h: (0, b, 0)),
                  head_block(0),
                  pl.BlockSpec((w, V7X_LANES), lambda b, h: (h, 0))],
        out_specs=head_block(0),
        out_shape=jax.ShapeDtypeStruct((B, S, M_WIDTH), BF16),
        scratch_shapes=scratch,
        compiler_params=pltpu.CompilerParams(dimension_semantics=("parallel", "arbitrary"),
                                             vmem_limit_bytes=_vmem_limit(est)),
        name="mlstm",
    )(mqk, mqk, mvt, gates_t, og, head_gain)


def _merge_ffn_body(x_ref, att_ref, mem_ref, ga_ref, gm_ref, wa_ref, wm_ref, wo_ref,
                    ng_ref, wg_ref, wu_ref, wd_ref, fin_ref, o_ref, *, apply_final_norm):
    ya = jnp.dot(att_ref[...], wa_ref[...], preferred_element_type=F32)
    ym = jnp.dot(mem_ref[...], wm_ref[...], preferred_element_type=F32)
    merged = (ga_ref[...].astype(F32) * ya + gm_ref[...].astype(F32) * ym).astype(BF16)
    x2 = x_ref[...] + jnp.dot(merged, wo_ref[...], preferred_element_type=F32)
    y = _swiglu_half_step(x2, ng_ref, wg_ref, wu_ref, wd_ref)
    if apply_final_norm:
        y = _rms_norm(y, fin_ref[...])
    o_ref[...] = y


def _merge_ffn(x2d, att, mem, ga, gm, w_ua, w_um, w_out, norm_g, w_gate, w_up, w_down, final_g, apply_final_norm):
    T = x2d.shape[0]
    tm = TOKEN_TILE

    def row(width):
        return pl.BlockSpec((tm, width), lambda i: (i, 0))

    est = (_ffn_vmem_bytes(tm) + (ATT_WIDTH + M_WIDTH + D_MODEL) * D_MODEL * 2
           + 2 * tm * (ATT_WIDTH + M_WIDTH + 2 * D_MODEL) * 2 + 4 * tm * D_MODEL * 4)
    return pl.pallas_call(
        functools.partial(_merge_ffn_body, apply_final_norm=apply_final_norm),
        grid=(T // tm,),
        in_specs=[row(D_MODEL), row(ATT_WIDTH), row(M_WIDTH), row(D_MODEL), row(D_MODEL),
                  _const_spec((ATT_WIDTH, D_MODEL)), _const_spec((M_WIDTH, D_MODEL)), _const_spec((D_MODEL, D_MODEL))]
        + _ffn_weight_specs() + [_const_spec((1, D_MODEL))],
        out_specs=row(D_MODEL),
        out_shape=jax.ShapeDtypeStruct((T, D_MODEL), F32),
        compiler_params=pltpu.CompilerParams(dimension_semantics=("parallel",), vmem_limit_bytes=_vmem_limit(est)),
        name="merge_ffn",
    )(x2d, att, mem, ga, gm, w_ua, w_um, w_out, norm_g, w_gate, w_up, w_down, final_g)


def _rope_tables(seq_len):
    t = np.arange(seq_len)
    inv_freq = ROPE_THETA ** (-np.arange(0, ROPE_AXIS_DIM, 2, dtype=np.float32) / ROPE_AXIS_DIM)
    ang = np.concatenate([(t // GRID_W).astype(np.float32)[:, None] * inv_freq,
                          (t % GRID_W).astype(np.float32)[:, None] * inv_freq], axis=-1).astype(np.float32)
    ang = jnp.asarray(ang)
    cos = jnp.repeat(jnp.cos(ang), 2, axis=-1)
    sin = jnp.repeat(jnp.sin(ang), 2, axis=-1) * jnp.tile(jnp.asarray([-1.0, 1.0], F32), ATT_HEAD_DIM // 2)
    return jnp.tile(cos, (1, 2)), jnp.tile(sin, (1, 2))


def _layer_params(l, ffn1_norm, ffn1_w_gate, ffn1_w_up, ffn1_w_down, mix_norm, w_in, q_norm, k_norm,
                  mlstm_gate_bias, mlstm_head_norm, w_up_att, w_up_mlstm, w_out,
                  ffn2_norm, ffn2_w_gate, ffn2_w_up, ffn2_w_down):
    offs = np.concatenate([[0], np.cumsum(IN_SPLITS)])
    w = w_in[l].astype(BF16)
    seg = [w[:, offs[i]:offs[i + 1]] for i in range(len(IN_SPLITS))]
    return dict(
        ffn1=(ffn1_norm[l][None], ffn1_w_gate[l].astype(BF16), ffn1_w_up[l].astype(BF16), ffn1_w_down[l].astype(BF16)),
        ffn2=(ffn2_norm[l][None], ffn2_w_gate[l].astype(BF16), ffn2_w_up[l].astype(BF16), ffn2_w_down[l].astype(BF16)),
        mix_norm=mix_norm[l][None],
        w_qkv=jnp.concatenate(seg[0:3], axis=1),
        w_m=jnp.concatenate(seg[3:5], axis=1),
        w_t=jnp.concatenate([seg[5].T, seg[6].T], axis=0),
        w_sg=jnp.concatenate(seg[7:10], axis=1),
        q_gain=jnp.tile(q_norm[l], 2)[None],
        k_gain=jnp.tile(k_norm[l], 2)[None],
        gate_bias=jnp.broadcast_to(mlstm_gate_bias[l][:, None], (N_GATE_COLS, TOKEN_TILE)),
        head_gain=jnp.broadcast_to(mlstm_head_norm[l][:, None], (M_WIDTH, V7X_LANES)),
        w_ua=w_up_att[l].astype(BF16), w_um=w_up_mlstm[l].astype(BF16), w_out=w_out[l].astype(BF16),
    )


def _trunk(x, layers, final_g):
    B, S, D = x.shape
    cos_t, sin_t = _rope_tables(S)
    x2d = x.reshape(B * S, D)
    for i, p in enumerate(layers):
        last = i == len(layers) - 1
        x1 = _ffn(x2d, *p["ffn1"])
        q, kd, vd, mqk, mvt, gates_t, og, ga, gm = _proj(x1, S, p["mix_norm"], p["w_qkv"], p["w_m"], p["w_t"], p["w_sg"],
                                                         p["q_gain"], p["k_gain"], cos_t, sin_t, p["gate_bias"])
        att = _attention(q.reshape(B, S, -1), kd.reshape(B, S, -1), vd.reshape(B, S, -1))
        mem = _mlstm(mqk.reshape(B, S, -1), mvt, gates_t.reshape(N_GATE_COLS, -1, M_CHUNK), og.reshape(B, S, -1),
                     p["head_gain"])
        x2d = _merge_ffn(x1, att.reshape(B * S, -1), mem.reshape(B * S, -1), ga, gm, p["w_ua"], p["w_um"], p["w_out"],
                         *p["ffn2"], final_g, last)
    return x2d.reshape(B, S, D)


def kernel(x_prompt, x_sample, ffn1_norm, ffn1_w_gate, ffn1_w_up, ffn1_w_down, mix_norm, w_in, q_norm, k_norm,
           mlstm_gate_bias, mlstm_head_norm, w_up_att, w_up_mlstm, w_out,
           ffn2_norm, ffn2_w_gate, ffn2_w_up, ffn2_w_down, final_norm):
    depth = w_in.shape[0]
    layers = [_layer_params(l, ffn1_norm, ffn1_w_gate, ffn1_w_up, ffn1_w_down, mix_norm, w_in, q_norm, k_norm,
                            mlstm_gate_bias, mlstm_head_norm, w_up_att, w_up_mlstm, w_out,
                            ffn2_norm, ffn2_w_gate, ffn2_w_up, ffn2_w_down) for l in range(depth)]
    final_g = final_norm[None]
    return _trunk(x_prompt, layers, final_g), _trunk(x_sample, layers, final_g)
```

```python
import functools

import jax
import jax.numpy as jnp
import numpy as np
from jax import lax
from jax.experimental import pallas as pl
from jax.experimental.pallas import tpu as pltpu

D_MODEL = 1024
D_FF = 2816
GRID_W = 64
EPS = 1e-6
ATT_HEADS = 8
ATT_KV_HEADS = 2
ATT_HEAD_DIM = 64
ROPE_AXIS_DIM = ATT_HEAD_DIM // 2
ROPE_THETA = 10000.0
M_HEADS = 4
M_HEAD_DIM = 128
ATT_WIDTH = ATT_HEADS * ATT_HEAD_DIM
KV_WIDTH = ATT_KV_HEADS * ATT_HEAD_DIM
M_WIDTH = M_HEADS * M_HEAD_DIM
N_GATE_COLS = 4 * M_HEADS
IN_SPLITS = (ATT_WIDTH, KV_WIDTH, KV_WIDTH, M_WIDTH, M_WIDTH, M_WIDTH, N_GATE_COLS, M_WIDTH, D_MODEL, D_MODEL)

V7X_LANES = 128
V7X_SUBLANES = 8
V7X_VMEM_BYTES = 64 * 1024 * 1024

TOKEN_TILE = 512
ATT_Q_TILE = 512
ATT_KEY_BLOCK = 1024
M_CHUNK = 128
M_HEADS_PER_STEP = 4
M_STEP_UNROLL = 4

ATT_Q_SCALE = ATT_HEAD_DIM ** -0.5 * float(np.log2(np.e))

F32 = jnp.float32
BF16 = jnp.bfloat16
NT_DIMS = (((1,), (1,)), ((), ()))


def _vmem_limit(nbytes):
    return int(min(nbytes * 1.25 + (4 << 20), V7X_VMEM_BYTES - (6 << 20)))


def _rms_norm(x, g):
    return x * lax.rsqrt(jnp.mean(x * x, axis=-1, keepdims=True) + EPS) * g


def _const_spec(shape):
    return pl.BlockSpec(shape, lambda *_: (0,) * len(shape), pipeline_mode=pl.Buffered(1))


def _swiglu_half_step(x, ng_ref, wg_ref, wu_ref, wd_ref):
    xn = _rms_norm(x, ng_ref[...]).astype(BF16)
    g = jnp.dot(xn, wg_ref[...], preferred_element_type=F32)
    u = jnp.dot(xn, wu_ref[...], preferred_element_type=F32)
    a = (g * jax.nn.sigmoid(g) * u).astype(BF16)
    return x + 0.5 * jnp.dot(a, wd_ref[...], preferred_element_type=F32)


def _ffn_vmem_bytes(tm):
    return 3 * D_MODEL * D_FF * 2 + 4 * tm * D_MODEL * 4 + tm * D_FF * (4 + 4 + 2) + tm * D_MODEL * 8


def _ffn_weight_specs():
    return [_const_spec((1, D_MODEL)), _const_spec((D_MODEL, D_FF)), _const_spec((D_MODEL, D_FF)),
            _const_spec((D_FF, D_MODEL))]


def _ffn_body(x_ref, ng_ref, wg_ref, wu_ref, wd_ref, o_ref):
    o_ref[...] = _swiglu_half_step(x_ref[...], ng_ref, wg_ref, wu_ref, wd_ref)


def _ffn(x2d, norm_g, w_gate, w_up, w_down):
    T = x2d.shape[0]
    tm = TOKEN_TILE
    row = pl.BlockSpec((tm, D_MODEL), lambda i: (i, 0))
    return pl.pallas_call(
        _ffn_body,
        grid=(T // tm,),
        in_specs=[row] + _ffn_weight_specs(),
        out_specs=row,
        out_shape=jax.ShapeDtypeStruct((T, D_MODEL), F32),
        compiler_params=pltpu.CompilerParams(dimension_semantics=("parallel",),
                                             vmem_limit_bytes=_vmem_limit(_ffn_vmem_bytes(tm))),
        name="ffn",
    )(x2d, norm_g, w_gate, w_up, w_down)


def _norm_rope(x, gain, cos, sin_signed, lo, even):
    x2 = x * x
    ss_lo = jnp.sum(jnp.where(lo, x2, 0.0), axis=-1, keepdims=True)
    ss_hi = jnp.sum(jnp.where(lo, 0.0, x2), axis=-1, keepdims=True)
    inv = jnp.where(lo, lax.rsqrt(ss_lo / ATT_HEAD_DIM + EPS), lax.rsqrt(ss_hi / ATT_HEAD_DIM + EPS))
    y = x * inv * gain
    partner = jnp.where(even, pltpu.roll(y, V7X_LANES - 1, 1), pltpu.roll(y, 1, 1))
    return y * cos + partner * sin_signed


def _dup_halves(x, lo):
    xr = pltpu.roll(x, ATT_HEAD_DIM, 1)
    return jnp.where(lo, x, xr), jnp.where(lo, xr, x)


def _proj_body(x_ref, ng_ref, wqkv_ref, wm_ref, wt_ref, wsg_ref, qg_ref, kg_ref, cos_ref, sin_ref, bias_ref,
               q_ref, kd_ref, vd_ref, mqk_ref, mvt_ref, gt_ref, og_ref, ga_ref, gm_ref):
    h = _rms_norm(x_ref[...], ng_ref[...]).astype(BF16)
    tm = h.shape[0]
    lane = lax.broadcasted_iota(jnp.int32, (tm, V7X_LANES), 1)
    lo = lane < ATT_HEAD_DIM
    even = (lane & 1) == 0
    cos = cos_ref[...]
    sin = sin_ref[...]

    qkv = jnp.dot(h, wqkv_ref[...], preferred_element_type=F32)
    for j in range(ATT_WIDTH // V7X_LANES):
        sl = slice(j * V7X_LANES, (j + 1) * V7X_LANES)
        qj = _norm_rope(qkv[:, sl], qg_ref[...], cos, sin, lo, even)
        q_ref[:, sl] = (qj * ATT_Q_SCALE).astype(BF16)
    k = _norm_rope(qkv[:, ATT_WIDTH:ATT_WIDTH + KV_WIDTH], kg_ref[...], cos, sin, lo, even)
    k0, k1 = _dup_halves(k, lo)
    kd_ref[:, :V7X_LANES] = k0.astype(BF16)
    kd_ref[:, V7X_LANES:] = k1.astype(BF16)
    v = qkv[:, ATT_WIDTH + KV_WIDTH:]
    vr = pltpu.roll(v, ATT_HEAD_DIM, 1)
    one_lo = jnp.where(lane == 0, 1.0, 0.0)
    one_hi = jnp.where(lane == ATT_HEAD_DIM, 1.0, 0.0)
    for j, val in enumerate((jnp.where(lo, v, one_hi), jnp.where(lo, one_lo, vr),
                             jnp.where(lo, vr, one_hi), jnp.where(lo, one_lo, v))):
        vd_ref[:, j * V7X_LANES:(j + 1) * V7X_LANES] = val.astype(BF16)

    m = jnp.dot(h, wm_ref[...], preferred_element_type=F32)
    mqk_ref[:, :M_WIDTH] = m[:, :M_WIDTH].astype(BF16)
    mqk_ref[:, M_WIDTH:] = (m[:, M_WIDTH:] * (M_HEAD_DIM ** -0.5)).astype(BF16)

    t = lax.dot_general(wt_ref[...], h, NT_DIMS, preferred_element_type=F32)
    for j in range(tm // M_CHUNK):
        mvt_ref[j] = t[:M_WIDTH, j * M_CHUNK:(j + 1) * M_CHUNK].astype(BF16)
    gt_ref[...] = t[M_WIDTH:, :] + bias_ref[...]

    sg = jax.nn.sigmoid(jnp.dot(h, wsg_ref[...], preferred_element_type=F32))
    og_ref[...] = sg[:, :M_WIDTH].astype(BF16)
    ga_ref[...] = sg[:, M_WIDTH:M_WIDTH + D_MODEL].astype(BF16)
    gm_ref[...] = sg[:, M_WIDTH + D_MODEL:].astype(BF16)


def _proj(x2d, seq_len, norm_g, w_qkv, w_m, w_t, w_sg, q_gain, k_gain, cos_t, sin_t, gate_bias):
    T = x2d.shape[0]
    tm = TOKEN_TILE
    tiles_per_seq = seq_len // tm
    n_qkv, n_m, n_t, n_sg = w_qkv.shape[1], w_m.shape[1], w_t.shape[0], w_sg.shape[1]

    def row(width):
        return pl.BlockSpec((tm, width), lambda i: (i, 0))

    rope = pl.BlockSpec((tm, V7X_LANES), lambda i: (i % tiles_per_seq, 0))
    outs = [
        (jax.ShapeDtypeStruct((T, ATT_WIDTH), BF16), row(ATT_WIDTH)),
        (jax.ShapeDtypeStruct((T, 2 * KV_WIDTH), BF16), row(2 * KV_WIDTH)),
        (jax.ShapeDtypeStruct((T, 4 * KV_WIDTH), BF16), row(4 * KV_WIDTH)),
        (jax.ShapeDtypeStruct((T, 2 * M_WIDTH), BF16), row(2 * M_WIDTH)),
        (jax.ShapeDtypeStruct((T // M_CHUNK, M_WIDTH, M_CHUNK), BF16),
         pl.BlockSpec((tm // M_CHUNK, M_WIDTH, M_CHUNK), lambda i: (i, 0, 0))),
        (jax.ShapeDtypeStruct((N_GATE_COLS, T), F32), pl.BlockSpec((N_GATE_COLS, tm), lambda i: (0, i))),
        (jax.ShapeDtypeStruct((T, M_WIDTH), BF16), row(M_WIDTH)),
        (jax.ShapeDtypeStruct((T, D_MODEL), BF16), row(D_MODEL)),
        (jax.ShapeDtypeStruct((T, D_MODEL), BF16), row(D_MODEL)),
    ]
    n_out_bf16 = ATT_WIDTH + 4 * KV_WIDTH + 4 * M_WIDTH + 2 * D_MODEL
    est = (D_MODEL * (n_qkv + n_m + n_t + n_sg) * 2 + 2 * tm * D_MODEL * 4
           + tm * (n_qkv + n_m + n_t + n_sg) * 6 + 2 * tm * n_out_bf16 * 2 + 8 * tm * V7X_LANES * 4)
    return pl.pallas_call(
        _proj_body,
        grid=(T // tm,),
        in_specs=[row(D_MODEL), _const_spec((1, D_MODEL)), _const_spec((D_MODEL, n_qkv)), _const_spec((D_MODEL, n_m)),
                  _const_spec((n_t, D_MODEL)), _const_spec((D_MODEL, n_sg)),
                  _const_spec((1, V7X_LANES)), _const_spec((1, V7X_LANES)), rope, rope,
                  _const_spec((N_GATE_COLS, tm))],
        out_specs=[s for _, s in outs],
        out_shape=[o for o, _ in outs],
        compiler_params=pltpu.CompilerParams(dimension_semantics=("parallel",), vmem_limit_bytes=_vmem_limit(est)),
        name="proj",
    )(x2d, norm_g, w_qkv, w_m, w_t, w_sg, q_gain, k_gain, cos_t, sin_t, gate_bias)


def _attn_body(q_ref, kd_ref, vd_ref, o_ref):
    tq = q_ref.shape[1]
    lane = lax.broadcasted_iota(jnp.int32, (tq, V7X_LANES), 1)
    lo = lane < ATT_HEAD_DIM
    pairs_per_kv = (ATT_HEADS // ATT_KV_HEADS) // 2
    n_key = kd_ref.shape[1]
    for g in range(ATT_KV_HEADS):
        for p in range(pairs_per_kv):
            c0 = (g * pairs_per_kv + p) * V7X_LANES
            q2 = q_ref[0, :, c0:c0 + V7X_LANES]
            outs = []
            for half, keep in enumerate((lo, jnp.logical_not(lo))):
                qm = jnp.where(keep, q2, jnp.zeros_like(q2))
                m_run, acc = None, None
                for k0 in range(0, n_key, ATT_KEY_BLOCK):
                    kd = kd_ref[0, k0:k0 + ATT_KEY_BLOCK, g * V7X_LANES:(g + 1) * V7X_LANES]
                    vd = vd_ref[0, k0:k0 + ATT_KEY_BLOCK, (2 * g + half) * V7X_LANES:(2 * g + half + 1) * V7X_LANES]
                    s = lax.dot_general(qm, kd, NT_DIMS, preferred_element_type=F32)
                    m_blk = jnp.max(s, axis=-1, keepdims=True)
                    m_new = m_blk if m_run is None else jnp.maximum(m_run, m_blk)
                    pv = jnp.dot(jnp.exp2(s - m_new).astype(BF16), vd, preferred_element_type=F32)
                    acc = pv if acc is None else jnp.exp2(m_run - m_new) * acc + pv
                    m_run = m_new
                d0 = (1 - half) * ATT_HEAD_DIM
                outs.append(acc * (1.0 / acc[:, d0:d0 + 1]))
            o_ref[0, :, c0:c0 + V7X_LANES] = jnp.where(lo, outs[0], outs[1]).astype(BF16)


def _attention(q, kd, vd):
    B, S, _ = q.shape
    tq = ATT_Q_TILE
    est = (2 * tq * ATT_WIDTH * 2 * 2 + 2 * S * 6 * KV_WIDTH * 2 + 3 * tq * S * 4)
    return pl.pallas_call(
        _attn_body,
        grid=(B, S // tq),
        in_specs=[pl.BlockSpec((1, tq, ATT_WIDTH), lambda b, i: (b, i, 0)),
                  pl.BlockSpec((1, S, 2 * KV_WIDTH), lambda b, i: (b, 0, 0)),
                  pl.BlockSpec((1, S, 4 * KV_WIDTH), lambda b, i: (b, 0, 0))],
        out_specs=pl.BlockSpec((1, tq, ATT_WIDTH), lambda b, i: (b, i, 0)),
        out_shape=jax.ShapeDtypeStruct((B, S, ATT_WIDTH), BF16),
        compiler_params=pltpu.CompilerParams(dimension_semantics=("parallel", "parallel"),
                                             vmem_limit_bytes=_vmem_limit(est)),
        name="attn",
    )(q, kd, vd)


def _split3(x):
    hi = x.astype(BF16)
    r = x - hi.astype(F32)
    mid = r.astype(BF16)
    lo = (r - mid.astype(F32)).astype(BF16)
    return hi, mid, lo


def _log_sigmoid(x):
    return jnp.minimum(x, 0.0) - jnp.log1p(jnp.exp(-jnp.abs(x)))


M_AUG = M_HEAD_DIM + 16


def _mlstm_body(q_ref, k_ref, vt_ref, g_ref, og_ref, hg_ref, o_ref,
                b_ref, x_ref, w_ref, bl_ref, gx_ref, ht_ref):
    L = M_CHUNK
    D = M_HEAD_DIM
    nc = vt_ref.shape[0]
    heads = q_ref.shape[2] // D
    head0 = pl.program_id(1) * heads
    streams = [(h, d) for h in range(heads) for d in range(2)]
    src = lax.broadcasted_iota(jnp.int32, (L, L), 0)
    dst = lax.broadcasted_iota(jnp.int32, (L, L), 1)
    visible = (src <= dst, src >= dst)
    tri = [jnp.where(v, 1.0, 0.0).astype(BF16) for v in visible]

    for s, (h, d) in enumerate(streams):
        li = g_ref[2 * d * M_HEADS + head0 + h]
        lf = _log_sigmoid(g_ref[(2 * d + 1) * M_HEADS + head0 + h])
        hi, mid, lo3 = _split3(lf)
        b = (jnp.dot(hi, tri[d], preferred_element_type=F32) + jnp.dot(mid, tri[d], preferred_element_type=F32)
             + jnp.dot(lo3, tri[d], preferred_element_type=F32))
        b_end = b[:, L - 1:L] if d == 0 else b[:, 0:1]
        g = b_end - b + li
        gmax = jnp.max(g, axis=1, keepdims=True)
        b_ref[s] = b
        x_ref[s] = li - b
        w_ref[s] = jnp.exp(g - gmax)
        bl_ref[s] = jnp.broadcast_to(b_end, (nc, L))
        gx_ref[s] = jnp.broadcast_to(gmax, (nc, L))

    ones_rows = jnp.where(lax.broadcasted_iota(jnp.int32, (M_AUG - D, L), 0) == 0, 1.0, 0.0).astype(BF16)

    def step(i, carry, *, complete):
        states, pending = carry
        for s in range(len(streams)):
            emit(i - 1, s, pending[s])
        if complete:
            write_out(i - 1)
            write_out(nc - i)
        nxt = [advance(i, s, states[s]) for s in range(len(streams))]
        return tuple(n[0] for n in nxt), tuple(n[1] for n in nxt)

    def chunk_of(i, d):
        return i if d == 0 else nc - 1 - i

    def value_rows(c, h):
        return jnp.concatenate([vt_ref[c, h * D:(h + 1) * D, :], ones_rows], axis=0)

    def advance(i, s, state):
        h, d = streams[s]
        ct, m = state
        c = chunk_of(i, d)
        r0 = c * L if isinstance(c, int) else pl.multiple_of(c * L, L)
        qc = q_ref[0, pl.ds(r0, L), h * D:(h + 1) * D]
        kc = k_ref[0, pl.ds(r0, L), h * D:(h + 1) * D]
        brow = b_ref[s, pl.ds(c, 1), :]
        xrow = x_ref[s, pl.ds(c, 1), :]
        wrow = w_ref[s, pl.ds(c, 1), :]
        b_end = bl_ref[s, pl.ds(c, 1), :]
        gmax = gx_ref[s, pl.ds(c, 1), :]

        ut = jnp.dot((value_rows(c, h).astype(F32) * wrow).astype(BF16), kc, preferred_element_type=F32)

        both = lax.dot_general(jnp.concatenate([kc, ct.astype(BF16)], axis=0), qc, NT_DIMS,
                               preferred_element_type=F32)
        xcol = jnp.broadcast_to(xrow, (L, L)).T
        dmat = jnp.where(visible[d], xcol + brow, -jnp.inf)
        a = jnp.max(dmat, axis=0, keepdims=True)
        pt = (both[:L] * jnp.exp(dmat - a)).astype(BF16)
        m_in = brow + m
        m_t = jnp.maximum(a, m_in)
        pend = (pt, both[L:], jnp.exp(a - m_t), jnp.exp(m_in - m_t), jnp.exp(-m_t))

        m_new = jnp.maximum(b_end + m, gmax)
        keep = jnp.exp(b_end + m - m_new)[:, :D]
        add = jnp.exp(gmax - m_new)[:, :D]
        return (keep * ct + add * ut, m_new), pend

    def emit(i, s, pend):
        h, d = streams[s]
        c = chunk_of(i, d)
        pt, inter, wi, wo, floor = pend
        intra = jnp.dot(value_rows(c, h), pt, preferred_element_type=F32)
        comb = wi * intra + wo * inter
        ht_ref[s, c] = comb[:D] * (1.0 / jnp.maximum(jnp.abs(comb[D:D + 1]), floor))

    def write_out(c):
        r0 = c * L if isinstance(c, int) else pl.multiple_of(c * L, L)
        for h in range(heads):
            ht = ht_ref[2 * h, c] + ht_ref[2 * h + 1, c]
            inv = lax.rsqrt(jnp.mean(ht * ht, axis=0, keepdims=True) + EPS)
            hn = (ht * inv * hg_ref[h * D:(h + 1) * D, 0:1]).T
            og = og_ref[0, pl.ds(r0, L), h * D:(h + 1) * D]
            o_ref[0, pl.ds(r0, L), h * D:(h + 1) * D] = (og.astype(F32) * hn).astype(BF16)

    assert nc % 2 == 0
    meet = nc // 2 + 1
    init = (jnp.zeros((M_AUG, D), F32), jnp.zeros((1, L), F32))
    first = [advance(0, s, init) for s in range(len(streams))]
    carry = (tuple(f[0] for f in first), tuple(f[1] for f in first))
    carry = lax.fori_loop(1, meet, functools.partial(step, complete=False), carry, unroll=M_STEP_UNROLL)
    _, pending = lax.fori_loop(meet, nc, functools.partial(step, complete=True), carry, unroll=M_STEP_UNROLL)
    for s in range(len(streams)):
        emit(nc - 1, s, pending[s])
    write_out(nc - 1)
    write_out(0)


def _mlstm(mqk, mvt, gates_t, og, head_gain):
    B, S, _ = mqk.shape
    L = M_CHUNK
    nc = S // L
    hb = M_HEADS_PER_STEP
    w = hb * M_HEAD_DIM
    groups = M_HEADS // hb

    def head_block(offset):
        return pl.BlockSpec((1, S, w), lambda b, h: (b, 0, offset + h))

    scratch = [pltpu.VMEM((2 * hb, nc, L), F32) for _ in range(5)] + [pltpu.VMEM((2 * hb, nc, M_HEAD_DIM, L), F32)]
    est = (2 * (2 * S * w * 2 + nc * w * L * 2 + N_GATE_COLS * nc * L * 4 + 2 * S * w * 2)
           + 2 * hb * nc * M_HEAD_DIM * L * 4 + 16 * M_AUG * L * 4 * 2 * hb)
    return pl.pallas_call(
        _mlstm_body,
        grid=(B, groups),
        in_specs=[head_block(0), head_block(groups),
                  pl.BlockSpec((nc, w, L), lambda b, h: (b, h, 0)),
                  pl.BlockSpec((N_GATE_COLS, nc, L), lambda b, h: (0, b, 0)),
                  head_block(0),
                  pl.BlockSpec((w, V7X_LANES), lambda b, h: (h, 0))],
        out_specs=head_block(0),
        out_shape=jax.ShapeDtypeStruct((B, S, M_WIDTH), BF16),
        scratch_shapes=scratch,
        compiler_params=pltpu.CompilerParams(dimension_semantics=("parallel", "arbitrary"),
                                             vmem_limit_bytes=_vmem_limit(est)),
        name="mlstm",
    )(mqk, mqk, mvt, gates_t, og, head_gain)


def _merge_ffn_body(x_ref, att_ref, mem_ref, ga_ref, gm_ref, wa_ref, wm_ref, wo_ref,
                    ng_ref, wg_ref, wu_ref, wd_ref, fin_ref, o_ref, *, apply_final_norm):
    ya = jnp.dot(att_ref[...], wa_ref[...], preferred_element_type=F32)
    ym = jnp.dot(mem_ref[...], wm_ref[...], preferred_element_type=F32)
    merged = (ga_ref[...].astype(F32) * ya + gm_ref[...].astype(F32) * ym).astype(BF16)
    x2 = x_ref[...] + jnp.dot(merged, wo_ref[...], preferred_element_type=F32)
    y = _swiglu_half_step(x2, ng_ref, wg_ref, wu_ref, wd_ref)
    if apply_final_norm:
        y = _rms_norm(y, fin_ref[...])
    o_ref[...] = y


def _merge_ffn(x2d, att, mem, ga, gm, w_ua, w_um, w_out, norm_g, w_gate, w_up, w_down, final_g, apply_final_norm):
    T = x2d.shape[0]
    tm = TOKEN_TILE

    def row(width):
        return pl.BlockSpec((tm, width), lambda i: (i, 0))

    est = (_ffn_vmem_bytes(tm) + (ATT_WIDTH + M_WIDTH + D_MODEL) * D_MODEL * 2
           + 2 * tm * (ATT_WIDTH + M_WIDTH + 2 * D_MODEL) * 2 + 4 * tm * D_MODEL * 4)
    return pl.pallas_call(
        functools.partial(_merge_ffn_body, apply_final_norm=apply_final_norm),
        grid=(T // tm,),
        in_specs=[row(D_MODEL), row(ATT_WIDTH), row(M_WIDTH), row(D_MODEL), row(D_MODEL),
                  _const_spec((ATT_WIDTH, D_MODEL)), _const_spec((M_WIDTH, D_MODEL)), _const_spec((D_MODEL, D_MODEL))]
        + _ffn_weight_specs() + [_const_spec((1, D_MODEL))],
        out_specs=row(D_MODEL),
        out_shape=jax.ShapeDtypeStruct((T, D_MODEL), F32),
        compiler_params=pltpu.CompilerParams(dimension_semantics=("parallel",), vmem_limit_bytes=_vmem_limit(est)),
        name="merge_ffn",
    )(x2d, att, mem, ga, gm, w_ua, w_um, w_out, norm_g, w_gate, w_up, w_down, final_g)


def _rope_tables(seq_len):
    t = np.arange(seq_len)
    inv_freq = ROPE_THETA ** (-np.arange(0, ROPE_AXIS_DIM, 2, dtype=np.float32) / ROPE_AXIS_DIM)
    ang = np.concatenate([(t // GRID_W).astype(np.float32)[:, None] * inv_freq,
                          (t % GRID_W).astype(np.float32)[:, None] * inv_freq], axis=-1).astype(np.float32)
    ang = jnp.asarray(ang)
    cos = jnp.repeat(jnp.cos(ang), 2, axis=-1)
    sin = jnp.repeat(jnp.sin(ang), 2, axis=-1) * jnp.tile(jnp.asarray([-1.0, 1.0], F32), ATT_HEAD_DIM // 2)
    return jnp.tile(cos, (1, 2)), jnp.tile(sin, (1, 2))


def _layer_params(l, ffn1_norm, ffn1_w_gate, ffn1_w_up, ffn1_w_down, mix_norm, w_in, q_norm, k_norm,
                  mlstm_gate_bias, mlstm_head_norm, w_up_att, w_up_mlstm, w_out,
                  ffn2_norm, ffn2_w_gate, ffn2_w_up, ffn2_w_down):
    offs = np.concatenate([[0], np.cumsum(IN_SPLITS)])
    w = w_in[l].astype(BF16)
    seg = [w[:, offs[i]:offs[i + 1]] for i in range(len(IN_SPLITS))]
    return dict(
        ffn1=(ffn1_norm[l][None], ffn1_w_gate[l].astype(BF16), ffn1_w_up[l].astype(BF16), ffn1_w_down[l].astype(BF16)),
        ffn2=(ffn2_norm[l][None], ffn2_w_gate[l].astype(BF16), ffn2_w_up[l].astype(BF16), ffn2_w_down[l].astype(BF16)),
        mix_norm=mix_norm[l][None],
        w_qkv=jnp.concatenate(seg[0:3], axis=1),
        w_m=jnp.concatenate(seg[3:5], axis=1),
        w_t=jnp.concatenate([seg[5].T, seg[6].T], axis=0),
        w_sg=jnp.concatenate(seg[7:10], axis=1),
        q_gain=jnp.tile(q_norm[l], 2)[None],
        k_gain=jnp.tile(k_norm[l], 2)[None],
        gate_bias=jnp.broadcast_to(mlstm_gate_bias[l][:, None], (N_GATE_COLS, TOKEN_TILE)),
        head_gain=jnp.broadcast_to(mlstm_head_norm[l][:, None], (M_WIDTH, V7X_LANES)),
        w_ua=w_up_att[l].astype(BF16), w_um=w_up_mlstm[l].astype(BF16), w_out=w_out[l].astype(BF16),
    )


def _trunk(x, layers, final_g):
    B, S, D = x.shape
    cos_t, sin_t = _rope_tables(S)
    x2d = x.reshape(B * S, D)
    for i, p in enumerate(layers):
        last = i == len(layers) - 1
        x1 = _ffn(x2d, *p["ffn1"])
        q, kd, vd, mqk, mvt, gates_t, og, ga, gm = _proj(x1, S, p["mix_norm"], p["w_qkv"], p["w_m"], p["w_t"], p["w_sg"],
                                                         p["q_gain"], p["k_gain"], cos_t, sin_t, p["gate_bias"])
        att = _attention(q.reshape(B, S, -1), kd.reshape(B, S, -1), vd.reshape(B, S, -1))
        mem = _mlstm(mqk.reshape(B, S, -1), mvt, gates_t.reshape(N_GATE_COLS, -1, M_CHUNK), og.reshape(B, S, -1),
                     p["head_gain"])
        x2d = _merge_ffn(x1, att.reshape(B * S, -1), mem.reshape(B * S, -1), ga, gm, p["w_ua"], p["w_um"], p["w_out"],
                         *p["ffn2"], final_g, last)
    return x2d.reshape(B, S, D)


def kernel(x_prompt, x_sample, ffn1_norm, ffn1_w_gate, ffn1_w_up, ffn1_w_down, mix_norm, w_in, q_norm, k_norm,
           mlstm_gate_bias, mlstm_head_norm, w_up_att, w_up_mlstm, w_out,
           ffn2_norm, ffn2_w_gate, ffn2_w_up, ffn2_w_down, final_norm):
    depth = w_in.shape[0]
    layers = [_layer_params(l, ffn1_norm, ffn1_w_gate, ffn1_w_up, ffn1_w_down, mix_norm, w_in, q_norm, k_norm,
                            mlstm_gate_bias, mlstm_head_norm, w_up_att, w_up_mlstm, w_out,
                            ffn2_norm, ffn2_w_gate, ffn2_w_up, ffn2_w_down) for l in range(depth)]
    final_g = final_norm[None]
    return _trunk(x_prompt, layers, final_g), _trunk(x_sample, layers, final_g)
```

```python
import functools

import jax
import jax.numpy as jnp
import numpy as np
from jax import lax
from jax.experimental import pallas as pl
from jax.experimental.pallas import tpu as pltpu

D_MODEL = 1024
D_FF = 2816
GRID_W = 64
EPS = 1e-6
ATT_HEADS = 8
ATT_KV_HEADS = 2
ATT_HEAD_DIM = 64
ROPE_AXIS_DIM = ATT_HEAD_DIM // 2
ROPE_THETA = 10000.0
M_HEADS = 4
M_HEAD_DIM = 128
ATT_WIDTH = ATT_HEADS * ATT_HEAD_DIM
KV_WIDTH = ATT_KV_HEADS * ATT_HEAD_DIM
M_WIDTH = M_HEADS * M_HEAD_DIM
N_GATE_COLS = 4 * M_HEADS
IN_SPLITS = (ATT_WIDTH, KV_WIDTH, KV_WIDTH, M_WIDTH, M_WIDTH, M_WIDTH, N_GATE_COLS, M_WIDTH, D_MODEL, D_MODEL)

V7X_LANES = 128
V7X_SUBLANES = 8
V7X_VMEM_BYTES = 64 * 1024 * 1024

TOKEN_TILE = 512
ATT_Q_TILE = 512
ATT_KEY_BLOCK = 1024
M_CHUNK = 128
M_HEADS_PER_STEP = 4
M_STEP_UNROLL = 8

ATT_Q_SCALE = ATT_HEAD_DIM ** -0.5 * float(np.log2(np.e))

F32 = jnp.float32
BF16 = jnp.bfloat16
NT_DIMS = (((1,), (1,)), ((), ()))


def _vmem_limit(nbytes):
    return int(min(nbytes * 1.25 + (4 << 20), V7X_VMEM_BYTES - (6 << 20)))


def _rms_norm(x, g):
    return x * lax.rsqrt(jnp.mean(x * x, axis=-1, keepdims=True) + EPS) * g


def _const_spec(shape):
    return pl.BlockSpec(shape, lambda *_: (0,) * len(shape), pipeline_mode=pl.Buffered(1))


def _swiglu_half_step(x, ng_ref, wg_ref, wu_ref, wd_ref):
    xn = _rms_norm(x, ng_ref[...]).astype(BF16)
    g = jnp.dot(xn, wg_ref[...], preferred_element_type=F32)
    u = jnp.dot(xn, wu_ref[...], preferred_element_type=F32)
    a = (g * jax.nn.sigmoid(g) * u).astype(BF16)
    return x + 0.5 * jnp.dot(a, wd_ref[...], preferred_element_type=F32)


def _ffn_vmem_bytes(tm):
    return 3 * D_MODEL * D_FF * 2 + 4 * tm * D_MODEL * 4 + tm * D_FF * (4 + 4 + 2) + tm * D_MODEL * 8


def _ffn_weight_specs():
    return [_const_spec((1, D_MODEL)), _const_spec((D_MODEL, D_FF)), _const_spec((D_MODEL, D_FF)),
            _const_spec((D_FF, D_MODEL))]


def _ffn_body(x_ref, ng_ref, wg_ref, wu_ref, wd_ref, o_ref):
    o_ref[...] = _swiglu_half_step(x_ref[...], ng_ref, wg_ref, wu_ref, wd_ref)


def _ffn(x2d, norm_g, w_gate, w_up, w_down):
    T = x2d.shape[0]
    tm = TOKEN_TILE
    row = pl.BlockSpec((tm, D_MODEL), lambda i: (i, 0))
    return pl.pallas_call(
        _ffn_body,
        grid=(T // tm,),
        in_specs=[row] + _ffn_weight_specs(),
        out_specs=row,
        out_shape=jax.ShapeDtypeStruct((T, D_MODEL), F32),
        compiler_params=pltpu.CompilerParams(dimension_semantics=("parallel",),
                                             vmem_limit_bytes=_vmem_limit(_ffn_vmem_bytes(tm))),
        name="ffn",
    )(x2d, norm_g, w_gate, w_up, w_down)


def _norm_rope(x, gain, cos, sin_signed, lo, even):
    x2 = x * x
    ss_lo = jnp.sum(jnp.where(lo, x2, 0.0), axis=-1, keepdims=True)
    ss_hi = jnp.sum(jnp.where(lo, 0.0, x2), axis=-1, keepdims=True)
    inv = jnp.where(lo, lax.rsqrt(ss_lo / ATT_HEAD_DIM + EPS), lax.rsqrt(ss_hi / ATT_HEAD_DIM + EPS))
    y = x * inv * gain
    partner = jnp.where(even, pltpu.roll(y, V7X_LANES - 1, 1), pltpu.roll(y, 1, 1))
    return y * cos + partner * sin_signed


def _dup_halves(x, lo):
    xr = pltpu.roll(x, ATT_HEAD_DIM, 1)
    return jnp.where(lo, x, xr), jnp.where(lo, xr, x)


def _proj_body(x_ref, ng_ref, wqkv_ref, wm_ref, wt_ref, wsg_ref, qg_ref, kg_ref, cos_ref, sin_ref, bias_ref,
               q_ref, kd_ref, vd_ref, mqk_ref, mvt_ref, gt_ref, og_ref, ga_ref, gm_ref):
    h = _rms_norm(x_ref[...], ng_ref[...]).astype(BF16)
    tm = h.shape[0]
    lane = lax.broadcasted_iota(jnp.int32, (tm, V7X_LANES), 1)
    lo = lane < ATT_HEAD_DIM
    even = (lane & 1) == 0
    cos = cos_ref[...]
    sin = sin_ref[...]

    qkv = jnp.dot(h, wqkv_ref[...], preferred_element_type=F32)
    for j in range(ATT_WIDTH // V7X_LANES):
        sl = slice(j * V7X_LANES, (j + 1) * V7X_LANES)
        qj = _norm_rope(qkv[:, sl], qg_ref[...], cos, sin, lo, even)
        q_ref[:, sl] = (qj * ATT_Q_SCALE).astype(BF16)
    k = _norm_rope(qkv[:, ATT_WIDTH:ATT_WIDTH + KV_WIDTH], kg_ref[...], cos, sin, lo, even)
    k0, k1 = _dup_halves(k, lo)
    kd_ref[:, :V7X_LANES] = k0.astype(BF16)
    kd_ref[:, V7X_LANES:] = k1.astype(BF16)
    v = qkv[:, ATT_WIDTH + KV_WIDTH:]
    vr = pltpu.roll(v, ATT_HEAD_DIM, 1)
    one_lo = jnp.where(lane == 0, 1.0, 0.0)
    one_hi = jnp.where(lane == ATT_HEAD_DIM, 1.0, 0.0)
    for j, val in enumerate((jnp.where(lo, v, one_hi), jnp.where(lo, one_lo, vr),
                             jnp.where(lo, vr, one_hi), jnp.where(lo, one_lo, v))):
        vd_ref[:, j * V7X_LANES:(j + 1) * V7X_LANES] = val.astype(BF16)

    m = jnp.dot(h, wm_ref[...], preferred_element_type=F32)
    mqk_ref[:, :M_WIDTH] = m[:, :M_WIDTH].astype(BF16)
    mqk_ref[:, M_WIDTH:] = (m[:, M_WIDTH:] * (M_HEAD_DIM ** -0.5)).astype(BF16)

    t = lax.dot_general(wt_ref[...], h, NT_DIMS, preferred_element_type=F32)
    for j in range(tm // M_CHUNK):
        mvt_ref[j] = t[:M_WIDTH, j * M_CHUNK:(j + 1) * M_CHUNK].astype(BF16)
    gt_ref[...] = t[M_WIDTH:, :] + bias_ref[...]

    sg = jax.nn.sigmoid(jnp.dot(h, wsg_ref[...], preferred_element_type=F32))
    og_ref[...] = sg[:, :M_WIDTH].astype(BF16)
    ga_ref[...] = sg[:, M_WIDTH:M_WIDTH + D_MODEL].astype(BF16)
    gm_ref[...] = sg[:, M_WIDTH + D_MODEL:].astype(BF16)


def _proj(x2d, seq_len, norm_g, w_qkv, w_m, w_t, w_sg, q_gain, k_gain, cos_t, sin_t, gate_bias):
    T = x2d.shape[0]
    tm = TOKEN_TILE
    tiles_per_seq = seq_len // tm
    n_qkv, n_m, n_t, n_sg = w_qkv.shape[1], w_m.shape[1], w_t.shape[0], w_sg.shape[1]

    def row(width):
        return pl.BlockSpec((tm, width), lambda i: (i, 0))

    rope = pl.BlockSpec((tm, V7X_LANES), lambda i: (i % tiles_per_seq, 0))
    outs = [
        (jax.ShapeDtypeStruct((T, ATT_WIDTH), BF16), row(ATT_WIDTH)),
        (jax.ShapeDtypeStruct((T, 2 * KV_WIDTH), BF16), row(2 * KV_WIDTH)),
        (jax.ShapeDtypeStruct((T, 4 * KV_WIDTH), BF16), row(4 * KV_WIDTH)),
        (jax.ShapeDtypeStruct((T, 2 * M_WIDTH), BF16), row(2 * M_WIDTH)),
        (jax.ShapeDtypeStruct((T // M_CHUNK, M_WIDTH, M_CHUNK), BF16),
         pl.BlockSpec((tm // M_CHUNK, M_WIDTH, M_CHUNK), lambda i: (i, 0, 0))),
        (jax.ShapeDtypeStruct((N_GATE_COLS, T), F32), pl.BlockSpec((N_GATE_COLS, tm), lambda i: (0, i))),
        (jax.ShapeDtypeStruct((T, M_WIDTH), BF16), row(M_WIDTH)),
        (jax.ShapeDtypeStruct((T, D_MODEL), BF16), row(D_MODEL)),
        (jax.ShapeDtypeStruct((T, D_MODEL), BF16), row(D_MODEL)),
    ]
    n_out_bf16 = ATT_WIDTH + 4 * KV_WIDTH + 4 * M_WIDTH + 2 * D_MODEL
    est = (D_MODEL * (n_qkv + n_m + n_t + n_sg) * 2 + 2 * tm * D_MODEL * 4
           + tm * (n_qkv + n_m + n_t + n_sg) * 6 + 2 * tm * n_out_bf16 * 2 + 8 * tm * V7X_LANES * 4)
    return pl.pallas_call(
        _proj_body,
        grid=(T // tm,),
        in_specs=[row(D_MODEL), _const_spec((1, D_MODEL)), _const_spec((D_MODEL, n_qkv)), _const_spec((D_MODEL, n_m)),
                  _const_spec((n_t, D_MODEL)), _const_spec((D_MODEL, n_sg)),
                  _const_spec((1, V7X_LANES)), _const_spec((1, V7X_LANES)), rope, rope,
                  _const_spec((N_GATE_COLS, tm))],
        out_specs=[s for _, s in outs],
        out_shape=[o for o, _ in outs],
        compiler_params=pltpu.CompilerParams(dimension_semantics=("parallel",), vmem_limit_bytes=_vmem_limit(est)),
        name="proj",
    )(x2d, norm_g, w_qkv, w_m, w_t, w_sg, q_gain, k_gain, cos_t, sin_t, gate_bias)


def _attn_body(q_ref, kd_ref, vd_ref, o_ref):
    tq = q_ref.shape[1]
    lane = lax.broadcasted_iota(jnp.int32, (tq, V7X_LANES), 1)
    lo = lane < ATT_HEAD_DIM
    pairs_per_kv = (ATT_HEADS // ATT_KV_HEADS) // 2
    n_key = kd_ref.shape[1]
    for g in range(ATT_KV_HEADS):
        for p in range(pairs_per_kv):
            c0 = (g * pairs_per_kv + p) * V7X_LANES
            q2 = q_ref[0, :, c0:c0 + V7X_LANES]
            outs = []
            for half, keep in enumerate((lo, jnp.logical_not(lo))):
                qm = jnp.where(keep, q2, jnp.zeros_like(q2))
                m_run, acc = None, None
                for k0 in range(0, n_key, ATT_KEY_BLOCK):
                    kd = kd_ref[0, k0:k0 + ATT_KEY_BLOCK, g * V7X_LANES:(g + 1) * V7X_LANES]
                    vd = vd_ref[0, k0:k0 + ATT_KEY_BLOCK, (2 * g + half) * V7X_LANES:(2 * g + half + 1) * V7X_LANES]
                    s = lax.dot_general(qm, kd, NT_DIMS, preferred_element_type=F32)
                    m_blk = jnp.max(s, axis=-1, keepdims=True)
                    m_new = m_blk if m_run is None else jnp.maximum(m_run, m_blk)
                    pv = jnp.dot(jnp.exp2(s - m_new).astype(BF16), vd, preferred_element_type=F32)
                    acc = pv if acc is None else jnp.exp2(m_run - m_new) * acc + pv
                    m_run = m_new
                d0 = (1 - half) * ATT_HEAD_DIM
                outs.append(acc * (1.0 / acc[:, d0:d0 + 1]))
            o_ref[0, :, c0:c0 + V7X_LANES] = jnp.where(lo, outs[0], outs[1]).astype(BF16)


def _attention(q, kd, vd):
    B, S, _ = q.shape
    tq = ATT_Q_TILE
    est = (2 * tq * ATT_WIDTH * 2 * 2 + 2 * S * 6 * KV_WIDTH * 2 + 3 * tq * S * 4)
    return pl.pallas_call(
        _attn_body,
        grid=(B, S // tq),
        in_specs=[pl.BlockSpec((1, tq, ATT_WIDTH), lambda b, i: (b, i, 0)),
                  pl.BlockSpec((1, S, 2 * KV_WIDTH), lambda b, i: (b, 0, 0)),
                  pl.BlockSpec((1, S, 4 * KV_WIDTH), lambda b, i: (b, 0, 0))],
        out_specs=pl.BlockSpec((1, tq, ATT_WIDTH), lambda b, i: (b, i, 0)),
        out_shape=jax.ShapeDtypeStruct((B, S, ATT_WIDTH), BF16),
        compiler_params=pltpu.CompilerParams(dimension_semantics=("parallel", "parallel"),
                                             vmem_limit_bytes=_vmem_limit(est)),
        name="attn",
    )(q, kd, vd)


def _split3(x):
    hi = x.astype(BF16)
    r = x - hi.astype(F32)
    mid = r.astype(BF16)
    lo = (r - mid.astype(F32)).astype(BF16)
    return hi, mid, lo


def _log_sigmoid(x):
    return jnp.minimum(x, 0.0) - jnp.log1p(jnp.exp(-jnp.abs(x)))


M_AUG = M_HEAD_DIM + 16


def _mlstm_body(q_ref, k_ref, vt_ref, g_ref, og_ref, hg_ref, o_ref,
                b_ref, x_ref, w_ref, bl_ref, gx_ref, ht_ref):
    L = M_CHUNK
    D = M_HEAD_DIM
    nc = vt_ref.shape[0]
    heads = q_ref.shape[2] // D
    head0 = pl.program_id(1) * heads
    streams = [(h, d) for h in range(heads) for d in range(2)]
    src = lax.broadcasted_iota(jnp.int32, (L, L), 0)
    dst = lax.broadcasted_iota(jnp.int32, (L, L), 1)
    visible = (src <= dst, src >= dst)
    tri = [jnp.where(v, 1.0, 0.0).astype(BF16) for v in visible]

    for s, (h, d) in enumerate(streams):
        li = g_ref[2 * d * M_HEADS + head0 + h]
        lf = _log_sigmoid(g_ref[(2 * d + 1) * M_HEADS + head0 + h])
        hi, mid, lo3 = _split3(lf)
        b = (jnp.dot(hi, tri[d], preferred_element_type=F32) + jnp.dot(mid, tri[d], preferred_element_type=F32)
             + jnp.dot(lo3, tri[d], preferred_element_type=F32))
        b_end = b[:, L - 1:L] if d == 0 else b[:, 0:1]
        g = b_end - b + li
        gmax = jnp.max(g, axis=1, keepdims=True)
        b_ref[s] = b
        x_ref[s] = li - b
        w_ref[s] = jnp.exp(g - gmax)
        bl_ref[s] = jnp.broadcast_to(b_end, (nc, L))
        gx_ref[s] = jnp.broadcast_to(gmax, (nc, L))

    ones_rows = jnp.where(lax.broadcasted_iota(jnp.int32, (M_AUG - D, L), 0) == 0, 1.0, 0.0).astype(BF16)

    def step(i, carry, *, complete):
        states, pending = carry
        for s in range(len(streams)):
            emit(i - 1, s, pending[s])
        if complete:
            write_out(i - 1)
            write_out(nc - i)
        nxt = [advance(i, s, states[s]) for s in range(len(streams))]
        return tuple(n[0] for n in nxt), tuple(n[1] for n in nxt)

    def chunk_of(i, d):
        return i if d == 0 else nc - 1 - i

    def value_rows(c, h):
        return jnp.concatenate([vt_ref[c, h * D:(h + 1) * D, :], ones_rows], axis=0)

    def advance(i, s, state):
        h, d = streams[s]
        ct, m = state
        c = chunk_of(i, d)
        r0 = c * L if isinstance(c, int) else pl.multiple_of(c * L, L)
        qc = q_ref[0, pl.ds(r0, L), h * D:(h + 1) * D]
        kc = k_ref[0, pl.ds(r0, L), h * D:(h + 1) * D]
        brow = b_ref[s, pl.ds(c, 1), :]
        xrow = x_ref[s, pl.ds(c, 1), :]
        wrow = w_ref[s, pl.ds(c, 1), :]
        b_end = bl_ref[s, pl.ds(c, 1), :]
        gmax = gx_ref[s, pl.ds(c, 1), :]

        ut = jnp.dot((value_rows(c, h).astype(F32) * wrow).astype(BF16), kc, preferred_element_type=F32)

        both = lax.dot_general(jnp.concatenate([kc, ct.astype(BF16)], axis=0), qc, NT_DIMS,
                               preferred_element_type=F32)
        xcol = jnp.broadcast_to(xrow, (L, L)).T
        dmat = jnp.where(visible[d], xcol + brow, -jnp.inf)
        a = jnp.max(dmat, axis=0, keepdims=True)
        pt = (both[:L] * jnp.exp(dmat - a)).astype(BF16)
        m_in = brow + m
        m_t = jnp.maximum(a, m_in)
        pend = (pt, both[L:], jnp.exp(a - m_t), jnp.exp(m_in - m_t), jnp.exp(-m_t))

        m_new = jnp.maximum(b_end + m, gmax)
        keep = jnp.exp(b_end + m - m_new)[:, :D]
        add = jnp.exp(gmax - m_new)[:, :D]
        return (keep * ct + add * ut, m_new), pend

    def emit(i, s, pend):
        h, d = streams[s]
        c = chunk_of(i, d)
        pt, inter, wi, wo, floor = pend
        intra = jnp.dot(value_rows(c, h), pt, preferred_element_type=F32)
        comb = wi * intra + wo * inter
        ht_ref[s, c] = comb[:D] * (1.0 / jnp.maximum(jnp.abs(comb[D:D + 1]), floor))

    def write_out(c):
        r0 = c * L if isinstance(c, int) else pl.multiple_of(c * L, L)
        for h in range(heads):
            ht = ht_ref[2 * h, c] + ht_ref[2 * h + 1, c]
            inv = lax.rsqrt(jnp.mean(ht * ht, axis=0, keepdims=True) + EPS)
            hn = (ht * inv * hg_ref[h * D:(h + 1) * D, 0:1]).T
            og = og_ref[0, pl.ds(r0, L), h * D:(h + 1) * D]
            o_ref[0, pl.ds(r0, L), h * D:(h + 1) * D] = (og.astype(F32) * hn).astype(BF16)

    assert nc % 2 == 0
    meet = nc // 2 + 1
    init = (jnp.zeros((M_AUG, D), F32), jnp.zeros((1, L), F32))
    first = [advance(0, s, init) for s in range(len(streams))]
    carry = (tuple(f[0] for f in first), tuple(f[1] for f in first))
    carry = lax.fori_loop(1, meet, functools.partial(step, complete=False), carry, unroll=M_STEP_UNROLL)
    _, pending = lax.fori_loop(meet, nc, functools.partial(step, complete=True), carry, unroll=M_STEP_UNROLL)
    for s in range(len(streams)):
        emit(nc - 1, s, pending[s])
    write_out(nc - 1)
    write_out(0)


def _mlstm(mqk, mvt, gates_t, og, head_gain):
    B, S, _ = mqk.shape
    L = M_CHUNK
    nc = S // L
    hb = M_HEADS_PER_STEP
    w = hb * M_HEAD_DIM
    groups = M_HEADS // hb

    def head_block(offset):
        return pl.BlockSpec((1, S, w), lambda b, h: (b, 0, offset + h))

    scratch = [pltpu.VMEM((2 * hb, nc, L), F32) for _ in range(5)] + [pltpu.VMEM((2 * hb, nc, M_HEAD_DIM, L), F32)]
    est = (2 * (2 * S * w * 2 + nc * w * L * 2 + N_GATE_COLS * nc * L * 4 + 2 * S * w * 2)
           + 2 * hb * nc * M_HEAD_DIM * L * 4 + 16 * M_AUG * L * 4 * 2 * hb)
    return pl.pallas_call(
        _mlstm_body,
        grid=(B, groups),
        in_specs=[head_block(0), head_block(groups),
                  pl.BlockSpec((nc, w, L), lambda b, h: (b, h, 0)),
                  pl.BlockSpec((N_GATE_COLS, nc, L), lambda b, h: (0, b, 0)),
                  head_block(0),
                  pl.BlockSpec((w, V7X_LANES), lambda b, h: (h, 0))],
        out_specs=head_block(0),
        out_shape=jax.ShapeDtypeStruct((B, S, M_WIDTH), BF16),
        scratch_shapes=scratch,
        compiler_params=pltpu.CompilerParams(dimension_semantics=("parallel", "arbitrary"),
                                             vmem_limit_bytes=_vmem_limit(est)),
        name="mlstm",
    )(mqk, mqk, mvt, gates_t, og, head_gain)


def _merge_ffn_body(x_ref, att_ref, mem_ref, ga_ref, gm_ref, wa_ref, wm_ref, wo_ref,
                    ng_ref, wg_ref, wu_ref, wd_ref, fin_ref, o_ref, *, apply_final_norm):
    ya = jnp.dot(att_ref[...], wa_ref[...], preferred_element_type=F32)
    ym = jnp.dot(mem_ref[...], wm_ref[...], preferred_element_type=F32)
    merged = (ga_ref[...].astype(F32) * ya + gm_ref[...].astype(F32) * ym).astype(BF16)
    x2 = x_ref[...] + jnp.dot(merged, wo_ref[...], preferred_element_type=F32)
    y = _swiglu_half_step(x2, ng_ref, wg_ref, wu_ref, wd_ref)
    if apply_final_norm:
        y = _rms_norm(y, fin_ref[...])
    o_ref[...] = y


def _merge_ffn(x2d, att, mem, ga, gm, w_ua, w_um, w_out, norm_g, w_gate, w_up, w_down, final_g, apply_final_norm):
    T = x2d.shape[0]
    tm = TOKEN_TILE

    def row(width):
        return pl.BlockSpec((tm, width), lambda i: (i, 0))

    est = (_ffn_vmem_bytes(tm) + (ATT_WIDTH + M_WIDTH + D_MODEL) * D_MODEL * 2
           + 2 * tm * (ATT_WIDTH + M_WIDTH + 2 * D_MODEL) * 2 + 4 * tm * D_MODEL * 4)
    return pl.pallas_call(
        functools.partial(_merge_ffn_body, apply_final_norm=apply_final_norm),
        grid=(T // tm,),
        in_specs=[row(D_MODEL), row(ATT_WIDTH), row(M_WIDTH), row(D_MODEL), row(D_MODEL),
                  _const_spec((ATT_WIDTH, D_MODEL)), _const_spec((M_WIDTH, D_MODEL)), _const_spec((D_MODEL, D_MODEL))]
        + _ffn_weight_specs() + [_const_spec((1, D_MODEL))],
        out_specs=row(D_MODEL),
        out_shape=jax.ShapeDtypeStruct((T, D_MODEL), F32),
        compiler_params=pltpu.CompilerParams(dimension_semantics=("parallel",), vmem_limit_bytes=_vmem_limit(est)),
        name="merge_ffn",
    )(x2d, att, mem, ga, gm, w_ua, w_um, w_out, norm_g, w_gate, w_up, w_down, final_g)


def _rope_tables(seq_len):
    t = np.arange(seq_len)
    inv_freq = ROPE_THETA ** (-np.arange(0, ROPE_AXIS_DIM, 2, dtype=np.float32) / ROPE_AXIS_DIM)
    ang = np.concatenate([(t // GRID_W).astype(np.float32)[:, None] * inv_freq,
                          (t % GRID_W).astype(np.float32)[:, None] * inv_freq], axis=-1).astype(np.float32)
    ang = jnp.asarray(ang)
    cos = jnp.repeat(jnp.cos(ang), 2, axis=-1)
    sin = jnp.repeat(jnp.sin(ang), 2, axis=-1) * jnp.tile(jnp.asarray([-1.0, 1.0], F32), ATT_HEAD_DIM // 2)
    return jnp.tile(cos, (1, 2)), jnp.tile(sin, (1, 2))


def _layer_params(l, ffn1_norm, ffn1_w_gate, ffn1_w_up, ffn1_w_down, mix_norm, w_in, q_norm, k_norm,
                  mlstm_gate_bias, mlstm_head_norm, w_up_att, w_up_mlstm, w_out,
                  ffn2_norm, ffn2_w_gate, ffn2_w_up, ffn2_w_down):
    offs = np.concatenate([[0], np.cumsum(IN_SPLITS)])

    def cols(first, last):
        return w_in[l][:, offs[first]:offs[last]].astype(BF16)

    return dict(
        ffn1=(ffn1_norm[l][None], ffn1_w_gate[l].astype(BF16), ffn1_w_up[l].astype(BF16), ffn1_w_down[l].astype(BF16)),
        ffn2=(ffn2_norm[l][None], ffn2_w_gate[l].astype(BF16), ffn2_w_up[l].astype(BF16), ffn2_w_down[l].astype(BF16)),
        mix_norm=mix_norm[l][None],
        w_qkv=cols(0, 3),
        w_m=cols(3, 5),
        w_t=cols(5, 7).T,
        w_sg=cols(7, 10),
        q_gain=jnp.tile(q_norm[l], 2)[None],
        k_gain=jnp.tile(k_norm[l], 2)[None],
        gate_bias=jnp.broadcast_to(mlstm_gate_bias[l][:, None], (N_GATE_COLS, TOKEN_TILE)),
        head_gain=jnp.broadcast_to(mlstm_head_norm[l][:, None], (M_WIDTH, V7X_LANES)),
        w_ua=w_up_att[l].astype(BF16), w_um=w_up_mlstm[l].astype(BF16), w_out=w_out[l].astype(BF16),
    )


def _trunk(x, layers, final_g):
    B, S, D = x.shape
    cos_t, sin_t = _rope_tables(S)
    x2d = x.reshape(B * S, D)
    for i, p in enumerate(layers):
        last = i == len(layers) - 1
        x1 = _ffn(x2d, *p["ffn1"])
        q, kd, vd, mqk, mvt, gates_t, og, ga, gm = _proj(x1, S, p["mix_norm"], p["w_qkv"], p["w_m"], p["w_t"], p["w_sg"],
                                                         p["q_gain"], p["k_gain"], cos_t, sin_t, p["gate_bias"])
        att = _attention(q.reshape(B, S, -1), kd.reshape(B, S, -1), vd.reshape(B, S, -1))
        mem = _mlstm(mqk.reshape(B, S, -1), mvt, gates_t.reshape(N_GATE_COLS, -1, M_CHUNK), og.reshape(B, S, -1),
                     p["head_gain"])
        x2d = _merge_ffn(x1, att.reshape(B * S, -1), mem.reshape(B * S, -1), ga, gm, p["w_ua"], p["w_um"], p["w_out"],
                         *p["ffn2"], final_g, last)
    return x2d.reshape(B, S, D)


def kernel(x_prompt, x_sample, ffn1_norm, ffn1_w_gate, ffn1_w_up, ffn1_w_down, mix_norm, w_in, q_norm, k_norm,
           mlstm_gate_bias, mlstm_head_norm, w_up_att, w_up_mlstm, w_out,
           ffn2_norm, ffn2_w_gate, ffn2_w_up, ffn2_w_down, final_norm):
    depth = w_in.shape[0]
    layers = [_layer_params(l, ffn1_norm, ffn1_w_gate, ffn1_w_up, ffn1_w_down, mix_norm, w_in, q_norm, k_norm,
                            mlstm_gate_bias, mlstm_head_norm, w_up_att, w_up_mlstm, w_out,
                            ffn2_norm, ffn2_w_gate, ffn2_w_up, ffn2_w_down) for l in range(depth)]
    final_g = final_norm[None]
    return _trunk(x_prompt, layers, final_g), _trunk(x_sample, layers, final_g)
```

```python
import functools

import jax
import jax.numpy as jnp
import numpy as np
from jax import lax
from jax.experimental import pallas as pl
from jax.experimental.pallas import tpu as pltpu

D_MODEL = 1024
D_FF = 2816
GRID_W = 64
EPS = 1e-6
ATT_HEADS = 8
ATT_KV_HEADS = 2
ATT_HEAD_DIM = 64
ROPE_AXIS_DIM = ATT_HEAD_DIM // 2
ROPE_THETA = 10000.0
M_HEADS = 4
M_HEAD_DIM = 128
ATT_WIDTH = ATT_HEADS * ATT_HEAD_DIM
KV_WIDTH = ATT_KV_HEADS * ATT_HEAD_DIM
M_WIDTH = M_HEADS * M_HEAD_DIM
N_GATE_COLS = 4 * M_HEADS
IN_SPLITS = (ATT_WIDTH, KV_WIDTH, KV_WIDTH, M_WIDTH, M_WIDTH, M_WIDTH, N_GATE_COLS, M_WIDTH, D_MODEL, D_MODEL)

V7X_LANES = 128
V7X_SUBLANES = 8
V7X_VMEM_BYTES = 64 * 1024 * 1024

TOKEN_TILE = 512
ATT_Q_TILE = 1024
ATT_KEY_BLOCK = 1024
M_CHUNK = 128
M_HEADS_PER_STEP = 4
M_STEP_UNROLL = 8

ATT_Q_SCALE = ATT_HEAD_DIM ** -0.5 * float(np.log2(np.e))

F32 = jnp.float32
BF16 = jnp.bfloat16
NT_DIMS = (((1,), (1,)), ((), ()))


def _vmem_limit(nbytes):
    return int(min(nbytes * 1.25 + (4 << 20), V7X_VMEM_BYTES - (6 << 20)))


def _rms_norm(x, g):
    return x * lax.rsqrt(jnp.mean(x * x, axis=-1, keepdims=True) + EPS) * g


def _const_spec(shape):
    return pl.BlockSpec(shape, lambda *_: (0,) * len(shape), pipeline_mode=pl.Buffered(1))


def _swiglu_half_step(x, ng_ref, wg_ref, wu_ref, wd_ref):
    xn = _rms_norm(x, ng_ref[...]).astype(BF16)
    g = jnp.dot(xn, wg_ref[...], preferred_element_type=F32)
    u = jnp.dot(xn, wu_ref[...], preferred_element_type=F32)
    a = (g * jax.nn.sigmoid(g) * u).astype(BF16)
    return x + 0.5 * jnp.dot(a, wd_ref[...], preferred_element_type=F32)


def _ffn_vmem_bytes(tm):
    return 3 * D_MODEL * D_FF * 2 + 4 * tm * D_MODEL * 4 + tm * D_FF * (4 + 4 + 2) + tm * D_MODEL * 8


def _ffn_weight_specs():
    return [_const_spec((1, D_MODEL)), _const_spec((D_MODEL, D_FF)), _const_spec((D_MODEL, D_FF)),
            _const_spec((D_FF, D_MODEL))]


def _ffn_body(x_ref, ng_ref, wg_ref, wu_ref, wd_ref, o_ref):
    o_ref[...] = _swiglu_half_step(x_ref[...], ng_ref, wg_ref, wu_ref, wd_ref)


def _ffn(x2d, norm_g, w_gate, w_up, w_down):
    T = x2d.shape[0]
    tm = TOKEN_TILE
    row = pl.BlockSpec((tm, D_MODEL), lambda i: (i, 0))
    return pl.pallas_call(
        _ffn_body,
        grid=(T // tm,),
        in_specs=[row] + _ffn_weight_specs(),
        out_specs=row,
        out_shape=jax.ShapeDtypeStruct((T, D_MODEL), F32),
        compiler_params=pltpu.CompilerParams(dimension_semantics=("parallel",),
                                             vmem_limit_bytes=_vmem_limit(_ffn_vmem_bytes(tm))),
        name="ffn",
    )(x2d, norm_g, w_gate, w_up, w_down)


def _norm_rope(x, gain, cos, sin_signed, lo, even):
    x2 = x * x
    ss_lo = jnp.sum(jnp.where(lo, x2, 0.0), axis=-1, keepdims=True)
    ss_hi = jnp.sum(jnp.where(lo, 0.0, x2), axis=-1, keepdims=True)
    inv = jnp.where(lo, lax.rsqrt(ss_lo / ATT_HEAD_DIM + EPS), lax.rsqrt(ss_hi / ATT_HEAD_DIM + EPS))
    y = x * inv * gain
    partner = jnp.where(even, pltpu.roll(y, V7X_LANES - 1, 1), pltpu.roll(y, 1, 1))
    return y * cos + partner * sin_signed


def _dup_halves(x, lo):
    xr = pltpu.roll(x, ATT_HEAD_DIM, 1)
    return jnp.where(lo, x, xr), jnp.where(lo, xr, x)


def _proj_body(x_ref, ng_ref, wqkv_ref, wm_ref, wt_ref, wsg_ref, qg_ref, kg_ref, cos_ref, sin_ref, bias_ref,
               q_ref, kd_ref, vd_ref, mqk_ref, mvt_ref, gt_ref, og_ref, ga_ref, gm_ref):
    h = _rms_norm(x_ref[...], ng_ref[...]).astype(BF16)
    tm = h.shape[0]
    lane = lax.broadcasted_iota(jnp.int32, (tm, V7X_LANES), 1)
    lo = lane < ATT_HEAD_DIM
    even = (lane & 1) == 0
    cos = cos_ref[...]
    sin = sin_ref[...]

    qkv = jnp.dot(h, wqkv_ref[...], preferred_element_type=F32)
    for j in range(ATT_WIDTH // V7X_LANES):
        sl = slice(j * V7X_LANES, (j + 1) * V7X_LANES)
        qj = _norm_rope(qkv[:, sl], qg_ref[...], cos, sin, lo, even)
        q_ref[:, sl] = (qj * ATT_Q_SCALE).astype(BF16)
    k = _norm_rope(qkv[:, ATT_WIDTH:ATT_WIDTH + KV_WIDTH], kg_ref[...], cos, sin, lo, even)
    k0, k1 = _dup_halves(k, lo)
    kd_ref[:, :V7X_LANES] = k0.astype(BF16)
    kd_ref[:, V7X_LANES:] = k1.astype(BF16)
    v = qkv[:, ATT_WIDTH + KV_WIDTH:]
    vr = pltpu.roll(v, ATT_HEAD_DIM, 1)
    one_lo = jnp.where(lane == 0, 1.0, 0.0)
    one_hi = jnp.where(lane == ATT_HEAD_DIM, 1.0, 0.0)
    for j, val in enumerate((jnp.where(lo, v, one_hi), jnp.where(lo, one_lo, vr),
                             jnp.where(lo, vr, one_hi), jnp.where(lo, one_lo, v))):
        vd_ref[:, j * V7X_LANES:(j + 1) * V7X_LANES] = val.astype(BF16)

    m = jnp.dot(h, wm_ref[...], preferred_element_type=F32)
    mqk_ref[:, :M_WIDTH] = m[:, :M_WIDTH].astype(BF16)
    mqk_ref[:, M_WIDTH:] = (m[:, M_WIDTH:] * (M_HEAD_DIM ** -0.5)).astype(BF16)

    t = lax.dot_general(wt_ref[...], h, NT_DIMS, preferred_element_type=F32)
    for j in range(tm // M_CHUNK):
        mvt_ref[j] = t[:M_WIDTH, j * M_CHUNK:(j + 1) * M_CHUNK].astype(BF16)
    gt_ref[...] = t[M_WIDTH:, :] + bias_ref[...]

    sg = jax.nn.sigmoid(jnp.dot(h, wsg_ref[...], preferred_element_type=F32))
    og_ref[...] = sg[:, :M_WIDTH].astype(BF16)
    ga_ref[...] = sg[:, M_WIDTH:M_WIDTH + D_MODEL].astype(BF16)
    gm_ref[...] = sg[:, M_WIDTH + D_MODEL:].astype(BF16)


def _proj(x2d, seq_len, norm_g, w_qkv, w_m, w_t, w_sg, q_gain, k_gain, cos_t, sin_t, gate_bias):
    T = x2d.shape[0]
    tm = TOKEN_TILE
    tiles_per_seq = seq_len // tm
    n_qkv, n_m, n_t, n_sg = w_qkv.shape[1], w_m.shape[1], w_t.shape[0], w_sg.shape[1]

    def row(width):
        return pl.BlockSpec((tm, width), lambda i: (i, 0))

    rope = pl.BlockSpec((tm, V7X_LANES), lambda i: (i % tiles_per_seq, 0))
    outs = [
        (jax.ShapeDtypeStruct((T, ATT_WIDTH), BF16), row(ATT_WIDTH)),
        (jax.ShapeDtypeStruct((T, 2 * KV_WIDTH), BF16), row(2 * KV_WIDTH)),
        (jax.ShapeDtypeStruct((T, 4 * KV_WIDTH), BF16), row(4 * KV_WIDTH)),
        (jax.ShapeDtypeStruct((T, 2 * M_WIDTH), BF16), row(2 * M_WIDTH)),
        (jax.ShapeDtypeStruct((T // M_CHUNK, M_WIDTH, M_CHUNK), BF16),
         pl.BlockSpec((tm // M_CHUNK, M_WIDTH, M_CHUNK), lambda i: (i, 0, 0))),
        (jax.ShapeDtypeStruct((N_GATE_COLS, T), F32), pl.BlockSpec((N_GATE_COLS, tm), lambda i: (0, i))),
        (jax.ShapeDtypeStruct((T, M_WIDTH), BF16), row(M_WIDTH)),
        (jax.ShapeDtypeStruct((T, D_MODEL), BF16), row(D_MODEL)),
        (jax.ShapeDtypeStruct((T, D_MODEL), BF16), row(D_MODEL)),
    ]
    n_out_bf16 = ATT_WIDTH + 4 * KV_WIDTH + 4 * M_WIDTH + 2 * D_MODEL
    est = (D_MODEL * (n_qkv + n_m + n_t + n_sg) * 2 + 2 * tm * D_MODEL * 4
           + tm * (n_qkv + n_m + n_t + n_sg) * 6 + 2 * tm * n_out_bf16 * 2 + 8 * tm * V7X_LANES * 4)
    return pl.pallas_call(
        _proj_body,
        grid=(T // tm,),
        in_specs=[row(D_MODEL), _const_spec((1, D_MODEL)), _const_spec((D_MODEL, n_qkv)), _const_spec((D_MODEL, n_m)),
                  _const_spec((n_t, D_MODEL)), _const_spec((D_MODEL, n_sg)),
                  _const_spec((1, V7X_LANES)), _const_spec((1, V7X_LANES)), rope, rope,
                  _const_spec((N_GATE_COLS, tm))],
        out_specs=[s for _, s in outs],
        out_shape=[o for o, _ in outs],
        compiler_params=pltpu.CompilerParams(dimension_semantics=("parallel",), vmem_limit_bytes=_vmem_limit(est)),
        name="proj",
    )(x2d, norm_g, w_qkv, w_m, w_t, w_sg, q_gain, k_gain, cos_t, sin_t, gate_bias)


def _attn_body(q_ref, kd_ref, vd_ref, o_ref):
    tq = q_ref.shape[1]
    lane = lax.broadcasted_iota(jnp.int32, (tq, V7X_LANES), 1)
    lo = lane < ATT_HEAD_DIM
    pairs_per_kv = (ATT_HEADS // ATT_KV_HEADS) // 2
    n_key = kd_ref.shape[1]
    for g in range(ATT_KV_HEADS):
        for p in range(pairs_per_kv):
            c0 = (g * pairs_per_kv + p) * V7X_LANES
            q2 = q_ref[0, :, c0:c0 + V7X_LANES]
            outs = []
            for half, keep in enumerate((lo, jnp.logical_not(lo))):
                qm = jnp.where(keep, q2, jnp.zeros_like(q2))
                m_run, acc = None, None
                for k0 in range(0, n_key, ATT_KEY_BLOCK):
                    kd = kd_ref[0, k0:k0 + ATT_KEY_BLOCK, g * V7X_LANES:(g + 1) * V7X_LANES]
                    vd = vd_ref[0, k0:k0 + ATT_KEY_BLOCK, (2 * g + half) * V7X_LANES:(2 * g + half + 1) * V7X_LANES]
                    s = lax.dot_general(qm, kd, NT_DIMS, preferred_element_type=F32)
                    m_blk = jnp.max(s, axis=-1, keepdims=True)
                    m_new = m_blk if m_run is None else jnp.maximum(m_run, m_blk)
                    pv = jnp.dot(jnp.exp2(s - m_new).astype(BF16), vd, preferred_element_type=F32)
                    acc = pv if acc is None else jnp.exp2(m_run - m_new) * acc + pv
                    m_run = m_new
                d0 = (1 - half) * ATT_HEAD_DIM
                outs.append(acc * (1.0 / acc[:, d0:d0 + 1]))
            o_ref[0, :, c0:c0 + V7X_LANES] = jnp.where(lo, outs[0], outs[1]).astype(BF16)


def _attention(q, kd, vd):
    B, S, _ = q.shape
    tq = ATT_Q_TILE
    est = (2 * tq * ATT_WIDTH * 2 * 2 + 2 * S * 6 * KV_WIDTH * 2 + 3 * tq * S * 4)
    return pl.pallas_call(
        _attn_body,
        grid=(B, S // tq),
        in_specs=[pl.BlockSpec((1, tq, ATT_WIDTH), lambda b, i: (b, i, 0)),
                  pl.BlockSpec((1, S, 2 * KV_WIDTH), lambda b, i: (b, 0, 0)),
                  pl.BlockSpec((1, S, 4 * KV_WIDTH), lambda b, i: (b, 0, 0))],
        out_specs=pl.BlockSpec((1, tq, ATT_WIDTH), lambda b, i: (b, i, 0)),
        out_shape=jax.ShapeDtypeStruct((B, S, ATT_WIDTH), BF16),
        compiler_params=pltpu.CompilerParams(dimension_semantics=("parallel", "parallel"),
                                             vmem_limit_bytes=_vmem_limit(est)),
        name="attn",
    )(q, kd, vd)


def _split3(x):
    hi = x.astype(BF16)
    r = x - hi.astype(F32)
    mid = r.astype(BF16)
    lo = (r - mid.astype(F32)).astype(BF16)
    return hi, mid, lo


def _log_sigmoid(x):
    return jnp.minimum(x, 0.0) - jnp.log1p(jnp.exp(-jnp.abs(x)))


M_AUG = M_HEAD_DIM + 16


def _mlstm_body(q_ref, k_ref, vt_ref, g_ref, og_ref, hg_ref, o_ref,
                b_ref, x_ref, w_ref, bl_ref, gx_ref, ht_ref):
    L = M_CHUNK
    D = M_HEAD_DIM
    nc = vt_ref.shape[0]
    heads = q_ref.shape[2] // D
    head0 = pl.program_id(1) * heads
    streams = [(h, d) for h in range(heads) for d in range(2)]
    src = lax.broadcasted_iota(jnp.int32, (L, L), 0)
    dst = lax.broadcasted_iota(jnp.int32, (L, L), 1)
    visible = (src <= dst, src >= dst)
    tri = [jnp.where(v, 1.0, 0.0).astype(BF16) for v in visible]

    for s, (h, d) in enumerate(streams):
        li = g_ref[2 * d * M_HEADS + head0 + h]
        lf = _log_sigmoid(g_ref[(2 * d + 1) * M_HEADS + head0 + h])
        hi, mid, lo3 = _split3(lf)
        b = (jnp.dot(hi, tri[d], preferred_element_type=F32) + jnp.dot(mid, tri[d], preferred_element_type=F32)
             + jnp.dot(lo3, tri[d], preferred_element_type=F32))
        b_end = b[:, L - 1:L] if d == 0 else b[:, 0:1]
        g = b_end - b + li
        gmax = jnp.max(g, axis=1, keepdims=True)
        b_ref[s] = b
        x_ref[s] = li - b
        w_ref[s] = jnp.exp(g - gmax)
        bl_ref[s] = jnp.broadcast_to(b_end, (nc, L))
        gx_ref[s] = jnp.broadcast_to(gmax, (nc, L))

    ones_rows = jnp.where(lax.broadcasted_iota(jnp.int32, (M_AUG - D, L), 0) == 0, 1.0, 0.0).astype(BF16)

    def step(i, carry, *, complete):
        states, pending = carry
        for s in range(len(streams)):
            emit(i - 1, s, pending[s])
        if complete:
            write_out(i - 1)
            write_out(nc - i)
        nxt = [advance(i, s, states[s]) for s in range(len(streams))]
        return tuple(n[0] for n in nxt), tuple(n[1] for n in nxt)

    def chunk_of(i, d):
        return i if d == 0 else nc - 1 - i

    def value_rows(c, h):
        return jnp.concatenate([vt_ref[c, h * D:(h + 1) * D, :], ones_rows], axis=0)

    def advance(i, s, state):
        h, d = streams[s]
        ct, m = state
        c = chunk_of(i, d)
        r0 = c * L if isinstance(c, int) else pl.multiple_of(c * L, L)
        qc = q_ref[0, pl.ds(r0, L), h * D:(h + 1) * D]
        kc = k_ref[0, pl.ds(r0, L), h * D:(h + 1) * D]
        brow = b_ref[s, pl.ds(c, 1), :]
        xrow = x_ref[s, pl.ds(c, 1), :]
        wrow = w_ref[s, pl.ds(c, 1), :]
        b_end = bl_ref[s, pl.ds(c, 1), :]
        gmax = gx_ref[s, pl.ds(c, 1), :]

        ut = jnp.dot((value_rows(c, h).astype(F32) * wrow).astype(BF16), kc, preferred_element_type=F32)

        both = lax.dot_general(jnp.concatenate([kc, ct.astype(BF16)], axis=0), qc, NT_DIMS,
                               preferred_element_type=F32)
        xcol = jnp.broadcast_to(xrow, (L, L)).T
        dmat = jnp.where(visible[d], xcol + brow, -jnp.inf)
        a = jnp.max(dmat, axis=0, keepdims=True)
        pt = (both[:L] * jnp.exp(dmat - a)).astype(BF16)
        m_in = brow + m
        m_t = jnp.maximum(a, m_in)
        pend = (pt, both[L:], jnp.exp(a - m_t), jnp.exp(m_in - m_t), jnp.exp(-m_t))

        m_new = jnp.maximum(b_end + m, gmax)
        keep = jnp.exp(b_end + m - m_new)[:, :D]
        add = jnp.exp(gmax - m_new)[:, :D]
        return (keep * ct + add * ut, m_new), pend

    def emit(i, s, pend):
        h, d = streams[s]
        c = chunk_of(i, d)
        pt, inter, wi, wo, floor = pend
        intra = jnp.dot(value_rows(c, h), pt, preferred_element_type=F32)
        comb = wi * intra + wo * inter
        ht_ref[s, c] = comb[:D] * (1.0 / jnp.maximum(jnp.abs(comb[D:D + 1]), floor))

    def write_out(c):
        r0 = c * L if isinstance(c, int) else pl.multiple_of(c * L, L)
        for h in range(heads):
            ht = ht_ref[2 * h, c] + ht_ref[2 * h + 1, c]
            inv = lax.rsqrt(jnp.mean(ht * ht, axis=0, keepdims=True) + EPS)
            hn = (ht * inv * hg_ref[h * D:(h + 1) * D, 0:1]).T
            og = og_ref[0, pl.ds(r0, L), h * D:(h + 1) * D]
            o_ref[0, pl.ds(r0, L), h * D:(h + 1) * D] = (og.astype(F32) * hn).astype(BF16)

    assert nc % 2 == 0
    meet = nc // 2 + 1
    init = (jnp.zeros((M_AUG, D), F32), jnp.zeros((1, L), F32))
    first = [advance(0, s, init) for s in range(len(streams))]
    carry = (tuple(f[0] for f in first), tuple(f[1] for f in first))
    carry = lax.fori_loop(1, meet, functools.partial(step, complete=False), carry, unroll=M_STEP_UNROLL)
    _, pending = lax.fori_loop(meet, nc, functools.partial(step, complete=True), carry, unroll=M_STEP_UNROLL)
    for s in range(len(streams)):
        emit(nc - 1, s, pending[s])
    write_out(nc - 1)
    write_out(0)


def _mlstm(mqk, mvt, gates_t, og, head_gain):
    B, S, _ = mqk.shape
    L = M_CHUNK
    nc = S // L
    hb = M_HEADS_PER_STEP
    w = hb * M_HEAD_DIM
    groups = M_HEADS // hb

    def head_block(offset):
        return pl.BlockSpec((1, S, w), lambda b, h: (b, 0, offset + h))

    scratch = [pltpu.VMEM((2 * hb, nc, L), F32) for _ in range(5)] + [pltpu.VMEM((2 * hb, nc, M_HEAD_DIM, L), F32)]
    est = (2 * (2 * S * w * 2 + nc * w * L * 2 + N_GATE_COLS * nc * L * 4 + 2 * S * w * 2)
           + 2 * hb * nc * M_HEAD_DIM * L * 4 + 16 * M_AUG * L * 4 * 2 * hb)
    return pl.pallas_call(
        _mlstm_body,
        grid=(B, groups),
        in_specs=[head_block(0), head_block(groups),
                  pl.BlockSpec((nc, w, L), lambda b, h: (b, h, 0)),
                  pl.BlockSpec((N_GATE_COLS, nc, L), lambda b, h: (0, b, 0)),
                  head_block(0),
                  pl.BlockSpec((w, V7X_LANES), lambda b, h: (h, 0))],
        out_specs=head_block(0),
        out_shape=jax.ShapeDtypeStruct((B, S, M_WIDTH), BF16),
        scratch_shapes=scratch,
        compiler_params=pltpu.CompilerParams(dimension_semantics=("parallel", "arbitrary"),
                                             vmem_limit_bytes=_vmem_limit(est)),
        name="mlstm",
    )(mqk, mqk, mvt, gates_t, og, head_gain)


def _merge_ffn_body(x_ref, att_ref, mem_ref, ga_ref, gm_ref, wa_ref, wm_ref, wo_ref,
                    ng_ref, wg_ref, wu_ref, wd_ref, fin_ref, o_ref, *, apply_final_norm):
    ya = jnp.dot(att_ref[...], wa_ref[...], preferred_element_type=F32)
    ym = jnp.dot(mem_ref[...], wm_ref[...], preferred_element_type=F32)
    merged = (ga_ref[...].astype(F32) * ya + gm_ref[...].astype(F32) * ym).astype(BF16)
    x2 = x_ref[...] + jnp.dot(merged, wo_ref[...], preferred_element_type=F32)
    y = _swiglu_half_step(x2, ng_ref, wg_ref, wu_ref, wd_ref)
    if apply_final_norm:
        y = _rms_norm(y, fin_ref[...])
    o_ref[...] = y


def _merge_ffn(x2d, att, mem, ga, gm, w_ua, w_um, w_out, norm_g, w_gate, w_up, w_down, final_g, apply_final_norm):
    T = x2d.shape[0]
    tm = TOKEN_TILE

    def row(width):
        return pl.BlockSpec((tm, width), lambda i: (i, 0))

    est = (_ffn_vmem_bytes(tm) + (ATT_WIDTH + M_WIDTH + D_MODEL) * D_MODEL * 2
           + 2 * tm * (ATT_WIDTH + M_WIDTH + 2 * D_MODEL) * 2 + 4 * tm * D_MODEL * 4)
    return pl.pallas_call(
        functools.partial(_merge_ffn_body, apply_final_norm=apply_final_norm),
        grid=(T // tm,),
        in_specs=[row(D_MODEL), row(ATT_WIDTH), row(M_WIDTH), row(D_MODEL), row(D_MODEL),
                  _const_spec((ATT_WIDTH, D_MODEL)), _const_spec((M_WIDTH, D_MODEL)), _const_spec((D_MODEL, D_MODEL))]
        + _ffn_weight_specs() + [_const_spec((1, D_MODEL))],
        out_specs=row(D_MODEL),
        out_shape=jax.ShapeDtypeStruct((T, D_MODEL), F32),
        compiler_params=pltpu.CompilerParams(dimension_semantics=("parallel",), vmem_limit_bytes=_vmem_limit(est)),
        name="merge_ffn",
    )(x2d, att, mem, ga, gm, w_ua, w_um, w_out, norm_g, w_gate, w_up, w_down, final_g)


def _rope_tables(seq_len):
    t = np.arange(seq_len)
    inv_freq = ROPE_THETA ** (-np.arange(0, ROPE_AXIS_DIM, 2, dtype=np.float32) / ROPE_AXIS_DIM)
    ang = np.concatenate([(t // GRID_W).astype(np.float32)[:, None] * inv_freq,
                          (t % GRID_W).astype(np.float32)[:, None] * inv_freq], axis=-1).astype(np.float32)
    ang = jnp.asarray(ang)
    cos = jnp.repeat(jnp.cos(ang), 2, axis=-1)
    sin = jnp.repeat(jnp.sin(ang), 2, axis=-1) * jnp.tile(jnp.asarray([-1.0, 1.0], F32), ATT_HEAD_DIM // 2)
    return jnp.tile(cos, (1, 2)), jnp.tile(sin, (1, 2))


def _layer_params(l, ffn1_norm, ffn1_w_gate, ffn1_w_up, ffn1_w_down, mix_norm, w_in, q_norm, k_norm,
                  mlstm_gate_bias, mlstm_head_norm, w_up_att, w_up_mlstm, w_out,
                  ffn2_norm, ffn2_w_gate, ffn2_w_up, ffn2_w_down):
    offs = np.concatenate([[0], np.cumsum(IN_SPLITS)])

    def cols(first, last):
        return w_in[l][:, offs[first]:offs[last]].astype(BF16)

    return dict(
        ffn1=(ffn1_norm[l][None], ffn1_w_gate[l].astype(BF16), ffn1_w_up[l].astype(BF16), ffn1_w_down[l].astype(BF16)),
        ffn2=(ffn2_norm[l][None], ffn2_w_gate[l].astype(BF16), ffn2_w_up[l].astype(BF16), ffn2_w_down[l].astype(BF16)),
        mix_norm=mix_norm[l][None],
        w_qkv=cols(0, 3),
        w_m=cols(3, 5),
        w_t=cols(5, 7).T,
        w_sg=cols(7, 10),
        q_gain=jnp.tile(q_norm[l], 2)[None],
        k_gain=jnp.tile(k_norm[l], 2)[None],
        gate_bias=jnp.broadcast_to(mlstm_gate_bias[l][:, None], (N_GATE_COLS, TOKEN_TILE)),
        head_gain=jnp.broadcast_to(mlstm_head_norm[l][:, None], (M_WIDTH, V7X_LANES)),
        w_ua=w_up_att[l].astype(BF16), w_um=w_up_mlstm[l].astype(BF16), w_out=w_out[l].astype(BF16),
    )


def _trunk(x, layers, final_g):
    B, S, D = x.shape
    cos_t, sin_t = _rope_tables(S)
    x2d = x.reshape(B * S, D)
    for i, p in enumerate(layers):
        last = i == len(layers) - 1
        x1 = _ffn(x2d, *p["ffn1"])
        q, kd, vd, mqk, mvt, gates_t, og, ga, gm = _proj(x1, S, p["mix_norm"], p["w_qkv"], p["w_m"], p["w_t"], p["w_sg"],
                                                         p["q_gain"], p["k_gain"], cos_t, sin_t, p["gate_bias"])
        att = _attention(q.reshape(B, S, -1), kd.reshape(B, S, -1), vd.reshape(B, S, -1))
        mem = _mlstm(mqk.reshape(B, S, -1), mvt, gates_t.reshape(N_GATE_COLS, -1, M_CHUNK), og.reshape(B, S, -1),
                     p["head_gain"])
        x2d = _merge_ffn(x1, att.reshape(B * S, -1), mem.reshape(B * S, -1), ga, gm, p["w_ua"], p["w_um"], p["w_out"],
                         *p["ffn2"], final_g, last)
    return x2d.reshape(B, S, D)


def kernel(x_prompt, x_sample, ffn1_norm, ffn1_w_gate, ffn1_w_up, ffn1_w_down, mix_norm, w_in, q_norm, k_norm,
           mlstm_gate_bias, mlstm_head_norm, w_up_att, w_up_mlstm, w_out,
           ffn2_norm, ffn2_w_gate, ffn2_w_up, ffn2_w_down, final_norm):
    depth = w_in.shape[0]
    layers = [_layer_params(l, ffn1_norm, ffn1_w_gate, ffn1_w_up, ffn1_w_down, mix_norm, w_in, q_norm, k_norm,
                            mlstm_gate_bias, mlstm_head_norm, w_up_att, w_up_mlstm, w_out,
                            ffn2_norm, ffn2_w_gate, ffn2_w_up, ffn2_w_down) for l in range(depth)]
    final_g = final_norm[None]
    return _trunk(x_prompt, layers, final_g), _trunk(x_sample, layers, final_g)
```

```python
import functools

import jax
import jax.numpy as jnp
import numpy as np
from jax import lax
from jax.experimental import pallas as pl
from jax.experimental.pallas import tpu as pltpu

D_MODEL = 1024
D_FF = 2816
GRID_W = 64
EPS = 1e-6
ATT_HEADS = 8
ATT_KV_HEADS = 2
ATT_HEAD_DIM = 64
ROPE_AXIS_DIM = ATT_HEAD_DIM // 2
ROPE_THETA = 10000.0
M_HEADS = 4
M_HEAD_DIM = 128
ATT_WIDTH = ATT_HEADS * ATT_HEAD_DIM
KV_WIDTH = ATT_KV_HEADS * ATT_HEAD_DIM
M_WIDTH = M_HEADS * M_HEAD_DIM
N_GATE_COLS = 4 * M_HEADS
IN_SPLITS = (ATT_WIDTH, KV_WIDTH, KV_WIDTH, M_WIDTH, M_WIDTH, M_WIDTH, N_GATE_COLS, M_WIDTH, D_MODEL, D_MODEL)

V7X_LANES = 128
V7X_VMEM_BYTES = 64 * 1024 * 1024
BF16_TILE_ROWS = 16

TOKEN_TILE = 512
ATT_Q_TILE = 1024
ATT_KEY_BLOCK = 1024
M_CHUNK = 128
M_HEADS_PER_STEP = 4
M_STEP_UNROLL = 8

ATT_Q_SCALE = ATT_HEAD_DIM ** -0.5 * float(np.log2(np.e))

F32 = jnp.float32
BF16 = jnp.bfloat16
NT_DIMS = (((1,), (1,)), ((), ()))


def _vmem_limit(nbytes):
    return int(min(nbytes * 1.25 + (4 << 20), V7X_VMEM_BYTES - (6 << 20)))


def _rms_norm(x, g):
    return x * lax.rsqrt(jnp.mean(x * x, axis=-1, keepdims=True) + EPS) * g


def _const_spec(shape):
    return pl.BlockSpec(shape, lambda *_: (0,) * len(shape), pipeline_mode=pl.Buffered(1))


def _swiglu_half_step(x, ng_ref, wg_ref, wu_ref, wd_ref):
    xn = _rms_norm(x, ng_ref[...]).astype(BF16)
    g = jnp.dot(xn, wg_ref[...], preferred_element_type=F32)
    u = jnp.dot(xn, wu_ref[...], preferred_element_type=F32)
    a = (g * jax.nn.sigmoid(g) * u).astype(BF16)
    return x + 0.5 * jnp.dot(a, wd_ref[...], preferred_element_type=F32)


def _ffn_vmem_bytes(tm):
    return 3 * D_MODEL * D_FF * 2 + 4 * tm * D_MODEL * 4 + tm * D_FF * (4 + 4 + 2) + tm * D_MODEL * 8


def _ffn_weight_specs():
    return [_const_spec((1, D_MODEL)), _const_spec((D_MODEL, D_FF)), _const_spec((D_MODEL, D_FF)),
            _const_spec((D_FF, D_MODEL))]


def _ffn_body(x_ref, ng_ref, wg_ref, wu_ref, wd_ref, *refs):
    n_cast = len(refs) // 2
    o_ref = refs[n_cast]
    o_ref[...] = _swiglu_half_step(x_ref[...], ng_ref, wg_ref, wu_ref, wd_ref)
    for src_ref, dst_ref in zip(refs[:n_cast], refs[n_cast + 1:]):
        dst_ref[...] = src_ref[...].astype(BF16)


def _cast_blocking(rows, n_steps):
    for steps_per_block in range(1, n_steps + 1):
        n_blocks, rem = divmod(n_steps, steps_per_block)
        if rem == 0 and rows % n_blocks == 0 and (rows // n_blocks) % BF16_TILE_ROWS == 0:
            return rows // n_blocks, steps_per_block
    raise ValueError(f"cannot spread {rows} rows over {n_steps} grid steps")


def _ffn(x2d, norm_g, w_gate, w_up, w_down, to_bf16=()):
    T = x2d.shape[0]
    tm = TOKEN_TILE
    n_steps = T // tm
    row = pl.BlockSpec((tm, D_MODEL), lambda i: (i, 0))
    cast_specs, cast_bytes = [], 0
    for w in to_bf16:
        rows, per = _cast_blocking(w.shape[0], n_steps)
        cast_specs.append(pl.BlockSpec((rows, w.shape[1]), lambda i, per=per: (i // per, 0)))
        cast_bytes += 2 * rows * w.shape[1] * (4 + 2)
    out = pl.pallas_call(
        _ffn_body,
        grid=(n_steps,),
        in_specs=[row] + _ffn_weight_specs() + cast_specs,
        out_specs=[row] + cast_specs,
        out_shape=[jax.ShapeDtypeStruct((T, D_MODEL), F32)] + [jax.ShapeDtypeStruct(w.shape, BF16) for w in to_bf16],
        compiler_params=pltpu.CompilerParams(dimension_semantics=("parallel",),
                                             vmem_limit_bytes=_vmem_limit(_ffn_vmem_bytes(tm) + cast_bytes)),
        name="ffn",
    )(x2d, norm_g, w_gate, w_up, w_down, *to_bf16)
    return out[0], out[1:]


def _norm_rope(x, gain, cos, sin_signed, lo, even):
    x2 = x * x
    ss_lo = jnp.sum(jnp.where(lo, x2, 0.0), axis=-1, keepdims=True)
    ss_hi = jnp.sum(jnp.where(lo, 0.0, x2), axis=-1, keepdims=True)
    inv = jnp.where(lo, lax.rsqrt(ss_lo / ATT_HEAD_DIM + EPS), lax.rsqrt(ss_hi / ATT_HEAD_DIM + EPS))
    y = x * inv * gain
    partner = jnp.where(even, pltpu.roll(y, V7X_LANES - 1, 1), pltpu.roll(y, 1, 1))
    return y * cos + partner * sin_signed


def _dup_halves(x, lo):
    xr = pltpu.roll(x, ATT_HEAD_DIM, 1)
    return jnp.where(lo, x, xr), jnp.where(lo, xr, x)


def _proj_body(x_ref, ng_ref, wqkv_ref, wm_ref, wt_ref, wsg_ref, qg_ref, kg_ref, cos_ref, sin_ref, bias_ref,
               q_ref, kd_ref, vd_ref, mqk_ref, mvt_ref, gt_ref, og_ref, ga_ref, gm_ref):
    h = _rms_norm(x_ref[...], ng_ref[...]).astype(BF16)
    tm = h.shape[0]
    lane = lax.broadcasted_iota(jnp.int32, (tm, V7X_LANES), 1)
    lo = lane < ATT_HEAD_DIM
    even = (lane & 1) == 0
    cos = cos_ref[...]
    sin = sin_ref[...]

    qkv = jnp.dot(h, wqkv_ref[...], preferred_element_type=F32)
    for j in range(ATT_WIDTH // V7X_LANES):
        sl = slice(j * V7X_LANES, (j + 1) * V7X_LANES)
        qj = _norm_rope(qkv[:, sl], qg_ref[...], cos, sin, lo, even)
        q_ref[:, sl] = (qj * ATT_Q_SCALE).astype(BF16)
    k = _norm_rope(qkv[:, ATT_WIDTH:ATT_WIDTH + KV_WIDTH], kg_ref[...], cos, sin, lo, even)
    k0, k1 = _dup_halves(k, lo)
    kd_ref[:, :V7X_LANES] = k0.astype(BF16)
    kd_ref[:, V7X_LANES:] = k1.astype(BF16)
    v = qkv[:, ATT_WIDTH + KV_WIDTH:]
    vr = pltpu.roll(v, ATT_HEAD_DIM, 1)
    one_lo = jnp.where(lane == 0, 1.0, 0.0)
    one_hi = jnp.where(lane == ATT_HEAD_DIM, 1.0, 0.0)
    for j, val in enumerate((jnp.where(lo, v, one_hi), jnp.where(lo, one_lo, vr),
                             jnp.where(lo, vr, one_hi), jnp.where(lo, one_lo, v))):
        vd_ref[:, j * V7X_LANES:(j + 1) * V7X_LANES] = val.astype(BF16)

    m = jnp.dot(h, wm_ref[...], preferred_element_type=F32)
    mqk_ref[:, :M_WIDTH] = m[:, :M_WIDTH].astype(BF16)
    mqk_ref[:, M_WIDTH:] = (m[:, M_WIDTH:] * (M_HEAD_DIM ** -0.5)).astype(BF16)

    t = lax.dot_general(wt_ref[...], h, NT_DIMS, preferred_element_type=F32)
    for j in range(tm // M_CHUNK):
        mvt_ref[j] = t[:M_WIDTH, j * M_CHUNK:(j + 1) * M_CHUNK].astype(BF16)
    gt_ref[...] = t[M_WIDTH:, :] + bias_ref[...]

    sg = jax.nn.sigmoid(jnp.dot(h, wsg_ref[...], preferred_element_type=F32))
    og_ref[...] = sg[:, :M_WIDTH].astype(BF16)
    ga_ref[...] = sg[:, M_WIDTH:M_WIDTH + D_MODEL].astype(BF16)
    gm_ref[...] = sg[:, M_WIDTH + D_MODEL:].astype(BF16)


def _proj(x2d, seq_len, norm_g, w_qkv, w_m, w_t, w_sg, q_gain, k_gain, cos_t, sin_t, gate_bias):
    T = x2d.shape[0]
    tm = TOKEN_TILE
    tiles_per_seq = seq_len // tm
    n_qkv, n_m, n_t, n_sg = w_qkv.shape[1], w_m.shape[1], w_t.shape[0], w_sg.shape[1]

    def row(width):
        return pl.BlockSpec((tm, width), lambda i: (i, 0))

    rope = pl.BlockSpec((tm, V7X_LANES), lambda i: (i % tiles_per_seq, 0))
    outs = [
        (jax.ShapeDtypeStruct((T, ATT_WIDTH), BF16), row(ATT_WIDTH)),
        (jax.ShapeDtypeStruct((T, 2 * KV_WIDTH), BF16), row(2 * KV_WIDTH)),
        (jax.ShapeDtypeStruct((T, 4 * KV_WIDTH), BF16), row(4 * KV_WIDTH)),
        (jax.ShapeDtypeStruct((T, 2 * M_WIDTH), BF16), row(2 * M_WIDTH)),
        (jax.ShapeDtypeStruct((T // M_CHUNK, M_WIDTH, M_CHUNK), BF16),
         pl.BlockSpec((tm // M_CHUNK, M_WIDTH, M_CHUNK), lambda i: (i, 0, 0))),
        (jax.ShapeDtypeStruct((N_GATE_COLS, T), F32), pl.BlockSpec((N_GATE_COLS, tm), lambda i: (0, i))),
        (jax.ShapeDtypeStruct((T, M_WIDTH), BF16), row(M_WIDTH)),
        (jax.ShapeDtypeStruct((T, D_MODEL), BF16), row(D_MODEL)),
        (jax.ShapeDtypeStruct((T, D_MODEL), BF16), row(D_MODEL)),
    ]
    n_out_bf16 = ATT_WIDTH + 4 * KV_WIDTH + 4 * M_WIDTH + 2 * D_MODEL
    est = (D_MODEL * (n_qkv + n_m + n_t + n_sg) * 2 + 2 * tm * D_MODEL * 4
           + tm * (n_qkv + n_m + n_t + n_sg) * 6 + 2 * tm * n_out_bf16 * 2 + 8 * tm * V7X_LANES * 4)
    return pl.pallas_call(
        _proj_body,
        grid=(T // tm,),
        in_specs=[row(D_MODEL), _const_spec((1, D_MODEL)), _const_spec((D_MODEL, n_qkv)), _const_spec((D_MODEL, n_m)),
                  _const_spec((n_t, D_MODEL)), _const_spec((D_MODEL, n_sg)),
                  _const_spec((1, V7X_LANES)), _const_spec((1, V7X_LANES)), rope, rope,
                  _const_spec((N_GATE_COLS, tm))],
        out_specs=[s for _, s in outs],
        out_shape=[o for o, _ in outs],
        compiler_params=pltpu.CompilerParams(dimension_semantics=("parallel",), vmem_limit_bytes=_vmem_limit(est)),
        name="proj",
    )(x2d, norm_g, w_qkv, w_m, w_t, w_sg, q_gain, k_gain, cos_t, sin_t, gate_bias)


def _attn_body(q_ref, kd_ref, vd_ref, o_ref):
    tq = q_ref.shape[1]
    lane = lax.broadcasted_iota(jnp.int32, (tq, V7X_LANES), 1)
    lo = lane < ATT_HEAD_DIM
    pairs_per_kv = (ATT_HEADS // ATT_KV_HEADS) // 2
    n_key = kd_ref.shape[1]
    for g in range(ATT_KV_HEADS):
        for p in range(pairs_per_kv):
            c0 = (g * pairs_per_kv + p) * V7X_LANES
            q2 = q_ref[0, :, c0:c0 + V7X_LANES]
            outs = []
            for half, keep in enumerate((lo, jnp.logical_not(lo))):
                qm = jnp.where(keep, q2, jnp.zeros_like(q2))
                m_run, acc = None, None
                for k0 in range(0, n_key, ATT_KEY_BLOCK):
                    kd = kd_ref[0, k0:k0 + ATT_KEY_BLOCK, g * V7X_LANES:(g + 1) * V7X_LANES]
                    vd = vd_ref[0, k0:k0 + ATT_KEY_BLOCK, (2 * g + half) * V7X_LANES:(2 * g + half + 1) * V7X_LANES]
                    s = lax.dot_general(qm, kd, NT_DIMS, preferred_element_type=F32)
                    m_blk = jnp.max(s, axis=-1, keepdims=True)
                    m_new = m_blk if m_run is None else jnp.maximum(m_run, m_blk)
                    pv = jnp.dot(jnp.exp2(s - m_new).astype(BF16), vd, preferred_element_type=F32)
                    acc = pv if acc is None else jnp.exp2(m_run - m_new) * acc + pv
                    m_run = m_new
                d0 = (1 - half) * ATT_HEAD_DIM
                outs.append(acc * (1.0 / acc[:, d0:d0 + 1]))
            o_ref[0, :, c0:c0 + V7X_LANES] = jnp.where(lo, outs[0], outs[1]).astype(BF16)


def _attention(q, kd, vd):
    B, S, _ = q.shape
    tq = ATT_Q_TILE
    est = (2 * tq * ATT_WIDTH * 2 * 2 + 2 * S * 6 * KV_WIDTH * 2 + 3 * tq * S * 4)
    return pl.pallas_call(
        _attn_body,
        grid=(B, S // tq),
        in_specs=[pl.BlockSpec((1, tq, ATT_WIDTH), lambda b, i: (b, i, 0)),
                  pl.BlockSpec((1, S, 2 * KV_WIDTH), lambda b, i: (b, 0, 0)),
                  pl.BlockSpec((1, S, 4 * KV_WIDTH), lambda b, i: (b, 0, 0))],
        out_specs=pl.BlockSpec((1, tq, ATT_WIDTH), lambda b, i: (b, i, 0)),
        out_shape=jax.ShapeDtypeStruct((B, S, ATT_WIDTH), BF16),
        compiler_params=pltpu.CompilerParams(dimension_semantics=("parallel", "parallel"),
                                             vmem_limit_bytes=_vmem_limit(est)),
        name="attn",
    )(q, kd, vd)


def _split3(x):
    hi = x.astype(BF16)
    r = x - hi.astype(F32)
    mid = r.astype(BF16)
    lo = (r - mid.astype(F32)).astype(BF16)
    return hi, mid, lo


def _log_sigmoid(x):
    return jnp.minimum(x, 0.0) - jnp.log1p(jnp.exp(-jnp.abs(x)))


M_AUG = M_HEAD_DIM + 16


def _mlstm_body(q_ref, k_ref, vt_ref, g_ref, og_ref, hg_ref, o_ref,
                b_ref, x_ref, w_ref, bl_ref, gx_ref, ht_ref):
    L = M_CHUNK
    D = M_HEAD_DIM
    nc = vt_ref.shape[0]
    heads = q_ref.shape[2] // D
    head0 = pl.program_id(1) * heads
    streams = [(h, d) for h in range(heads) for d in range(2)]
    src = lax.broadcasted_iota(jnp.int32, (L, L), 0)
    dst = lax.broadcasted_iota(jnp.int32, (L, L), 1)
    visible = (src <= dst, src >= dst)
    tri = [jnp.where(v, 1.0, 0.0).astype(BF16) for v in visible]

    for s, (h, d) in enumerate(streams):
        li = g_ref[2 * d * M_HEADS + head0 + h]
        lf = _log_sigmoid(g_ref[(2 * d + 1) * M_HEADS + head0 + h])
        hi, mid, lo3 = _split3(lf)
        b = (jnp.dot(hi, tri[d], preferred_element_type=F32) + jnp.dot(mid, tri[d], preferred_element_type=F32)
             + jnp.dot(lo3, tri[d], preferred_element_type=F32))
        b_end = b[:, L - 1:L] if d == 0 else b[:, 0:1]
        g = b_end - b + li
        gmax = jnp.max(g, axis=1, keepdims=True)
        b_ref[s] = b
        x_ref[s] = li - b
        w_ref[s] = jnp.exp(g - gmax)
        bl_ref[s] = jnp.broadcast_to(b_end, (nc, L))
        gx_ref[s] = jnp.broadcast_to(gmax, (nc, L))

    ones_rows = jnp.where(lax.broadcasted_iota(jnp.int32, (M_AUG - D, L), 0) == 0, 1.0, 0.0).astype(BF16)

    def step(i, carry, *, complete):
        states, pending = carry
        for s in range(len(streams)):
            emit(i - 1, s, pending[s])
        if complete:
            write_out(i - 1)
            write_out(nc - i)
        nxt = [advance(i, s, states[s]) for s in range(len(streams))]
        return tuple(n[0] for n in nxt), tuple(n[1] for n in nxt)

    def chunk_of(i, d):
        return i if d == 0 else nc - 1 - i

    def value_rows(c, h):
        return jnp.concatenate([vt_ref[c, h * D:(h + 1) * D, :], ones_rows], axis=0)

    def advance(i, s, state):
        h, d = streams[s]
        ct, m = state
        c = chunk_of(i, d)
        r0 = c * L if isinstance(c, int) else pl.multiple_of(c * L, L)
        qc = q_ref[0, pl.ds(r0, L), h * D:(h + 1) * D]
        kc = k_ref[0, pl.ds(r0, L), h * D:(h + 1) * D]
        brow = b_ref[s, pl.ds(c, 1), :]
        xrow = x_ref[s, pl.ds(c, 1), :]
        wrow = w_ref[s, pl.ds(c, 1), :]
        b_end = bl_ref[s, pl.ds(c, 1), :]
        gmax = gx_ref[s, pl.ds(c, 1), :]

        ut = jnp.dot((value_rows(c, h).astype(F32) * wrow).astype(BF16), kc, preferred_element_type=F32)

        both = lax.dot_general(jnp.concatenate([kc, ct.astype(BF16)], axis=0), qc, NT_DIMS,
                               preferred_element_type=F32)
        xcol = jnp.broadcast_to(xrow, (L, L)).T
        dmat = jnp.where(visible[d], xcol + brow, -jnp.inf)
        a = jnp.max(dmat, axis=0, keepdims=True)
        pt = (both[:L] * jnp.exp(dmat - a)).astype(BF16)
        m_in = brow + m
        m_t = jnp.maximum(a, m_in)
        pend = (pt, both[L:], jnp.exp(a - m_t), jnp.exp(m_in - m_t), jnp.exp(-m_t))

        m_new = jnp.maximum(b_end + m, gmax)
        keep = jnp.exp(b_end + m - m_new)[:, :D]
        add = jnp.exp(gmax - m_new)[:, :D]
        return (keep * ct + add * ut, m_new), pend

    def emit(i, s, pend):
        h, d = streams[s]
        c = chunk_of(i, d)
        pt, inter, wi, wo, floor = pend
        intra = jnp.dot(value_rows(c, h), pt, preferred_element_type=F32)
        comb = wi * intra + wo * inter
        ht_ref[s, c] = comb[:D] * (1.0 / jnp.maximum(jnp.abs(comb[D:D + 1]), floor))

    def write_out(c):
        r0 = c * L if isinstance(c, int) else pl.multiple_of(c * L, L)
        for h in range(heads):
            ht = ht_ref[2 * h, c] + ht_ref[2 * h + 1, c]
            inv = lax.rsqrt(jnp.mean(ht * ht, axis=0, keepdims=True) + EPS)
            hn = (ht * inv * hg_ref[h * D:(h + 1) * D, 0:1]).T
            og = og_ref[0, pl.ds(r0, L), h * D:(h + 1) * D]
            o_ref[0, pl.ds(r0, L), h * D:(h + 1) * D] = (og.astype(F32) * hn).astype(BF16)

    assert nc % 2 == 0
    meet = nc // 2 + 1
    init = (jnp.zeros((M_AUG, D), F32), jnp.zeros((1, L), F32))
    first = [advance(0, s, init) for s in range(len(streams))]
    carry = (tuple(f[0] for f in first), tuple(f[1] for f in first))
    carry = lax.fori_loop(1, meet, functools.partial(step, complete=False), carry, unroll=M_STEP_UNROLL)
    _, pending = lax.fori_loop(meet, nc, functools.partial(step, complete=True), carry, unroll=M_STEP_UNROLL)
    for s in range(len(streams)):
        emit(nc - 1, s, pending[s])
    write_out(nc - 1)
    write_out(0)


def _mlstm(mqk, mvt, gates_t, og, head_gain):
    B, S, _ = mqk.shape
    L = M_CHUNK
    nc = S // L
    hb = M_HEADS_PER_STEP
    w = hb * M_HEAD_DIM
    groups = M_HEADS // hb

    def head_block(offset):
        return pl.BlockSpec((1, S, w), lambda b, h: (b, 0, offset + h))

    scratch = [pltpu.VMEM((2 * hb, nc, L), F32) for _ in range(5)] + [pltpu.VMEM((2 * hb, nc, M_HEAD_DIM, L), F32)]
    est = (2 * (2 * S * w * 2 + nc * w * L * 2 + N_GATE_COLS * nc * L * 4 + 2 * S * w * 2)
           + 2 * hb * nc * M_HEAD_DIM * L * 4 + 16 * M_AUG * L * 4 * 2 * hb)
    return pl.pallas_call(
        _mlstm_body,
        grid=(B, groups),
        in_specs=[head_block(0), head_block(groups),
                  pl.BlockSpec((nc, w, L), lambda b, h: (b, h, 0)),
                  pl.BlockSpec((N_GATE_COLS, nc, L), lambda b, h: (0, b, 0)),
                  head_block(0),
                  pl.BlockSpec((w, V7X_LANES), lambda b, h: (h, 0))],
        out_specs=head_block(0),
        out_shape=jax.ShapeDtypeStruct((B, S, M_WIDTH), BF16),
        scratch_shapes=scratch,
        compiler_params=pltpu.CompilerParams(dimension_semantics=("parallel", "arbitrary"),
                                             vmem_limit_bytes=_vmem_limit(est)),
        name="mlstm",
    )(mqk, mqk, mvt, gates_t, og, head_gain)


def _merge_ffn_body(x_ref, att_ref, mem_ref, ga_ref, gm_ref, wa_ref, wm_ref, wo_ref,
                    ng_ref, wg_ref, wu_ref, wd_ref, fin_ref, o_ref, *, apply_final_norm):
    ya = jnp.dot(att_ref[...], wa_ref[...], preferred_element_type=F32)
    ym = jnp.dot(mem_ref[...], wm_ref[...], preferred_element_type=F32)
    merged = (ga_ref[...].astype(F32) * ya + gm_ref[...].astype(F32) * ym).astype(BF16)
    x2 = x_ref[...] + jnp.dot(merged, wo_ref[...], preferred_element_type=F32)
    y = _swiglu_half_step(x2, ng_ref, wg_ref, wu_ref, wd_ref)
    if apply_final_norm:
        y = _rms_norm(y, fin_ref[...])
    o_ref[...] = y


def _merge_ffn(x2d, att, mem, ga, gm, w_ua, w_um, w_out, norm_g, w_gate, w_up, w_down, final_g, apply_final_norm):
    T = x2d.shape[0]
    tm = TOKEN_TILE

    def row(width):
        return pl.BlockSpec((tm, width), lambda i: (i, 0))

    est = (_ffn_vmem_bytes(tm) + (ATT_WIDTH + M_WIDTH + D_MODEL) * D_MODEL * 2
           + 2 * tm * (ATT_WIDTH + M_WIDTH + 2 * D_MODEL) * 2 + 4 * tm * D_MODEL * 4)
    return pl.pallas_call(
        functools.partial(_merge_ffn_body, apply_final_norm=apply_final_norm),
        grid=(T // tm,),
        in_specs=[row(D_MODEL), row(ATT_WIDTH), row(M_WIDTH), row(D_MODEL), row(D_MODEL),
                  _const_spec((ATT_WIDTH, D_MODEL)), _const_spec((M_WIDTH, D_MODEL)), _const_spec((D_MODEL, D_MODEL))]
        + _ffn_weight_specs() + [_const_spec((1, D_MODEL))],
        out_specs=row(D_MODEL),
        out_shape=jax.ShapeDtypeStruct((T, D_MODEL), F32),
        compiler_params=pltpu.CompilerParams(dimension_semantics=("parallel",), vmem_limit_bytes=_vmem_limit(est)),
        name="merge_ffn",
    )(x2d, att, mem, ga, gm, w_ua, w_um, w_out, norm_g, w_gate, w_up, w_down, final_g)


def _rope_tables(seq_len):
    t = np.arange(seq_len)
    inv_freq = ROPE_THETA ** (-np.arange(0, ROPE_AXIS_DIM, 2, dtype=np.float32) / ROPE_AXIS_DIM)
    ang = np.concatenate([(t // GRID_W).astype(np.float32)[:, None] * inv_freq,
                          (t % GRID_W).astype(np.float32)[:, None] * inv_freq], axis=-1).astype(np.float32)
    ang = jnp.asarray(ang)
    cos = jnp.repeat(jnp.cos(ang), 2, axis=-1)
    sin = jnp.repeat(jnp.sin(ang), 2, axis=-1) * jnp.tile(jnp.asarray([-1.0, 1.0], F32), ATT_HEAD_DIM // 2)
    return jnp.tile(cos, (1, 2)), jnp.tile(sin, (1, 2))


def _layer_params(l, ffn1, mix_norm, w_in, q_norm, k_norm, mlstm_gate_bias, mlstm_head_norm,
                  w_up_att, w_up_mlstm, w_out, ffn2_norm, ffn2_w_gate, ffn2_w_up, ffn2_w_down):
    offs = np.concatenate([[0], np.cumsum(IN_SPLITS)])

    def cols(first, last):
        return w_in[l][:, offs[first]:offs[last]]

    return dict(
        ffn1=ffn1,
        ffn2=(ffn2_norm[l][None], ffn2_w_gate[l], ffn2_w_up[l], ffn2_w_down[l]),
        mix_norm=mix_norm[l][None],
        w_qkv=cols(0, 3),
        w_m=cols(3, 5),
        w_t=cols(5, 7).T,
        w_sg=cols(7, 10),
        q_gain=jnp.tile(q_norm[l], 2)[None],
        k_gain=jnp.tile(k_norm[l], 2)[None],
        gate_bias=jnp.broadcast_to(mlstm_gate_bias[l][:, None], (N_GATE_COLS, TOKEN_TILE)),
        head_gain=jnp.broadcast_to(mlstm_head_norm[l][:, None], (M_WIDTH, V7X_LANES)),
        w_ua=w_up_att[l], w_um=w_up_mlstm[l], w_out=w_out[l],
    )


def _trunk(x, layers, final_g, x1_first=None):
    B, S, D = x.shape
    cos_t, sin_t = _rope_tables(S)
    x2d = x.reshape(B * S, D)
    for i, p in enumerate(layers):
        last = i == len(layers) - 1
        x1 = x1_first if i == 0 and x1_first is not None else _ffn(x2d, *p["ffn1"])[0]
        q, kd, vd, mqk, mvt, gates_t, og, ga, gm = _proj(x1, S, p["mix_norm"], p["w_qkv"], p["w_m"], p["w_t"], p["w_sg"],
                                                         p["q_gain"], p["k_gain"], cos_t, sin_t, p["gate_bias"])
        att = _attention(q.reshape(B, S, -1), kd.reshape(B, S, -1), vd.reshape(B, S, -1))
        mem = _mlstm(mqk.reshape(B, S, -1), mvt, gates_t.reshape(N_GATE_COLS, -1, M_CHUNK), og.reshape(B, S, -1),
                     p["head_gain"])
        x2d = _merge_ffn(x1, att.reshape(B * S, -1), mem.reshape(B * S, -1), ga, gm, p["w_ua"], p["w_um"], p["w_out"],
                         *p["ffn2"], final_g, last)
    return x2d.reshape(B, S, D)


def kernel(x_prompt, x_sample, ffn1_norm, ffn1_w_gate, ffn1_w_up, ffn1_w_down, mix_norm, w_in, q_norm, k_norm,
           mlstm_gate_bias, mlstm_head_norm, w_up_att, w_up_mlstm, w_out,
           ffn2_norm, ffn2_w_gate, ffn2_w_up, ffn2_w_down, final_norm):
    depth = w_in.shape[0]
    final_g = final_norm[None]
    ffn1 = [(ffn1_norm[l][None], ffn1_w_gate[l].astype(BF16), ffn1_w_up[l].astype(BF16), ffn1_w_down[l].astype(BF16))
            for l in range(depth)]
    later = (w_in, w_up_att, w_up_mlstm, w_out, ffn2_w_gate, ffn2_w_up, ffn2_w_down)
    B, S, D = x_prompt.shape
    x1_prompt, converted = _ffn(x_prompt.reshape(B * S, D), *ffn1[0],
                                to_bf16=[w.reshape(-1, w.shape[-1]) for w in later])
    w_in_b, w_ua_b, w_um_b, w_out_b, ffn2_g_b, ffn2_u_b, ffn2_d_b = (c.reshape(w.shape) for c, w in zip(converted, later))
    layers = [_layer_params(l, ffn1[l], mix_norm, w_in_b, q_norm, k_norm, mlstm_gate_bias, mlstm_head_norm,
                            w_ua_b, w_um_b, w_out_b, ffn2_norm, ffn2_g_b, ffn2_u_b, ffn2_d_b) for l in range(depth)]
    return _trunk(x_prompt, layers, final_g, x1_first=x1_prompt), _trunk(x_sample, layers, final_g)
```

```python
import functools

import jax
import jax.numpy as jnp
import numpy as np
from jax import lax
from jax.experimental import pallas as pl
from jax.experimental.pallas import tpu as pltpu

D_MODEL = 1024
D_FF = 2816
GRID_W = 64
EPS = 1e-6
ATT_HEADS = 8
ATT_KV_HEADS = 2
ATT_HEAD_DIM = 64
ROPE_AXIS_DIM = ATT_HEAD_DIM // 2
ROPE_THETA = 10000.0
M_HEADS = 4
M_HEAD_DIM = 128
ATT_WIDTH = ATT_HEADS * ATT_HEAD_DIM
KV_WIDTH = ATT_KV_HEADS * ATT_HEAD_DIM
M_WIDTH = M_HEADS * M_HEAD_DIM
N_GATE_COLS = 4 * M_HEADS
IN_SPLITS = (ATT_WIDTH, KV_WIDTH, KV_WIDTH, M_WIDTH, M_WIDTH, M_WIDTH, N_GATE_COLS, M_WIDTH, D_MODEL, D_MODEL)

V7X_LANES = 128
V7X_VMEM_BYTES = 64 * 1024 * 1024

TOKEN_TILE = 512
FFN_TILES_PER_STEP = 2
ATT_Q_TILE = 1024
ATT_KEY_BLOCK = 1024
M_CHUNK = 128
M_HEADS_PER_STEP = 4
M_STEP_UNROLL = 8

ATT_Q_SCALE = ATT_HEAD_DIM ** -0.5 * float(np.log2(np.e))

F32 = jnp.float32
BF16 = jnp.bfloat16
NT_DIMS = (((1,), (1,)), ((), ()))


def _vmem_limit(nbytes):
    return int(min(nbytes * 1.25 + (4 << 20), V7X_VMEM_BYTES - (6 << 20)))


def _rms_norm(x, g):
    return x * lax.rsqrt(jnp.mean(x * x, axis=-1, keepdims=True) + EPS) * g


def _const_spec(shape):
    return pl.BlockSpec(shape, lambda *_: (0,) * len(shape), pipeline_mode=pl.Buffered(1))


def _swiglu_half_step(x, ng_ref, wg_ref, wu_ref, wd_ref):
    xn = _rms_norm(x, ng_ref[...]).astype(BF16)
    g = jnp.dot(xn, wg_ref[...], preferred_element_type=F32)
    u = jnp.dot(xn, wu_ref[...], preferred_element_type=F32)
    a = (g * jax.nn.sigmoid(g) * u).astype(BF16)
    return x + 0.5 * jnp.dot(a, wd_ref[...], preferred_element_type=F32)


def _ffn_vmem_bytes(tm):
    return 3 * D_MODEL * D_FF * 2 + 4 * tm * D_MODEL * 4 + tm * D_FF * (4 + 4 + 2) + tm * D_MODEL * 8


def _ffn_weight_specs():
    return [_const_spec((1, D_MODEL)), _const_spec((D_MODEL, D_FF)), _const_spec((D_MODEL, D_FF)),
            _const_spec((D_FF, D_MODEL))]


def _ffn_body(x_ref, ng_ref, wg_ref, wu_ref, wd_ref, o_ref):
    for r in range(0, x_ref.shape[0], TOKEN_TILE):
        rows = slice(r, r + TOKEN_TILE)
        o_ref[rows, :] = _swiglu_half_step(x_ref[rows, :], ng_ref, wg_ref, wu_ref, wd_ref)


def _ffn(x2d, norm_g, w_gate, w_up, w_down):
    T = x2d.shape[0]
    tm = TOKEN_TILE
    rows = FFN_TILES_PER_STEP * tm
    row = pl.BlockSpec((rows, D_MODEL), lambda i: (i, 0))
    est = _ffn_vmem_bytes(tm) + 4 * (rows - tm) * D_MODEL * 4 + tm * D_FF * (4 + 4 + 2)
    return pl.pallas_call(
        _ffn_body,
        grid=(T // rows,),
        in_specs=[row] + _ffn_weight_specs(),
        out_specs=row,
        out_shape=jax.ShapeDtypeStruct((T, D_MODEL), F32),
        compiler_params=pltpu.CompilerParams(dimension_semantics=("parallel",), vmem_limit_bytes=_vmem_limit(est)),
        name="ffn",
    )(x2d, norm_g, w_gate, w_up, w_down)


def _norm_rope(x, gain, cos, sin_signed, lo, even):
    x2 = x * x
    ss_lo = jnp.sum(jnp.where(lo, x2, 0.0), axis=-1, keepdims=True)
    ss_hi = jnp.sum(jnp.where(lo, 0.0, x2), axis=-1, keepdims=True)
    inv = jnp.where(lo, lax.rsqrt(ss_lo / ATT_HEAD_DIM + EPS), lax.rsqrt(ss_hi / ATT_HEAD_DIM + EPS))
    y = x * inv * gain
    partner = jnp.where(even, pltpu.roll(y, V7X_LANES - 1, 1), pltpu.roll(y, 1, 1))
    return y * cos + partner * sin_signed


def _dup_halves(x, lo):
    xr = pltpu.roll(x, ATT_HEAD_DIM, 1)
    return jnp.where(lo, x, xr), jnp.where(lo, xr, x)


def _proj_body(x_ref, ng_ref, wqkv_ref, wm_ref, wt_ref, wsg_ref, qg_ref, kg_ref, cos_ref, sin_ref, bias_ref,
               q_ref, kd_ref, vd_ref, mqk_ref, mvt_ref, gt_ref, og_ref, ga_ref, gm_ref):
    h = _rms_norm(x_ref[...], ng_ref[...]).astype(BF16)
    tm = h.shape[0]
    lane = lax.broadcasted_iota(jnp.int32, (tm, V7X_LANES), 1)
    lo = lane < ATT_HEAD_DIM
    even = (lane & 1) == 0
    cos = cos_ref[...]
    sin = sin_ref[...]

    qkv = jnp.dot(h, wqkv_ref[...], preferred_element_type=F32)
    for j in range(ATT_WIDTH // V7X_LANES):
        sl = slice(j * V7X_LANES, (j + 1) * V7X_LANES)
        qj = _norm_rope(qkv[:, sl], qg_ref[...], cos, sin, lo, even)
        q_ref[:, sl] = (qj * ATT_Q_SCALE).astype(BF16)
    k = _norm_rope(qkv[:, ATT_WIDTH:ATT_WIDTH + KV_WIDTH], kg_ref[...], cos, sin, lo, even)
    k0, k1 = _dup_halves(k, lo)
    kd_ref[:, :V7X_LANES] = k0.astype(BF16)
    kd_ref[:, V7X_LANES:] = k1.astype(BF16)
    v = qkv[:, ATT_WIDTH + KV_WIDTH:]
    vr = pltpu.roll(v, ATT_HEAD_DIM, 1)
    one_lo = jnp.where(lane == 0, 1.0, 0.0)
    one_hi = jnp.where(lane == ATT_HEAD_DIM, 1.0, 0.0)
    for j, val in enumerate((jnp.where(lo, v, one_hi), jnp.where(lo, one_lo, vr),
                             jnp.where(lo, vr, one_hi), jnp.where(lo, one_lo, v))):
        vd_ref[:, j * V7X_LANES:(j + 1) * V7X_LANES] = val.astype(BF16)

    m = jnp.dot(h, wm_ref[...], preferred_element_type=F32)
    mqk_ref[:, :M_WIDTH] = m[:, :M_WIDTH].astype(BF16)
    mqk_ref[:, M_WIDTH:] = (m[:, M_WIDTH:] * (M_HEAD_DIM ** -0.5)).astype(BF16)

    t = lax.dot_general(wt_ref[...], h, NT_DIMS, preferred_element_type=F32)
    for j in range(tm // M_CHUNK):
        mvt_ref[j] = t[:M_WIDTH, j * M_CHUNK:(j + 1) * M_CHUNK].astype(BF16)
    gt_ref[...] = t[M_WIDTH:, :] + bias_ref[...]

    sg = jax.nn.sigmoid(jnp.dot(h, wsg_ref[...], preferred_element_type=F32))
    og_ref[...] = sg[:, :M_WIDTH].astype(BF16)
    ga_ref[...] = sg[:, M_WIDTH:M_WIDTH + D_MODEL].astype(BF16)
    gm_ref[...] = sg[:, M_WIDTH + D_MODEL:].astype(BF16)


def _proj(x2d, seq_len, norm_g, w_qkv, w_m, w_t, w_sg, q_gain, k_gain, cos_t, sin_t, gate_bias):
    T = x2d.shape[0]
    tm = TOKEN_TILE
    tiles_per_seq = seq_len // tm
    n_qkv, n_m, n_t, n_sg = w_qkv.shape[1], w_m.shape[1], w_t.shape[0], w_sg.shape[1]

    def row(width):
        return pl.BlockSpec((tm, width), lambda i: (i, 0))

    rope = pl.BlockSpec((tm, V7X_LANES), lambda i: (i % tiles_per_seq, 0))
    outs = [
        (jax.ShapeDtypeStruct((T, ATT_WIDTH), BF16), row(ATT_WIDTH)),
        (jax.ShapeDtypeStruct((T, 2 * KV_WIDTH), BF16), row(2 * KV_WIDTH)),
        (jax.ShapeDtypeStruct((T, 4 * KV_WIDTH), BF16), row(4 * KV_WIDTH)),
        (jax.ShapeDtypeStruct((T, 2 * M_WIDTH), BF16), row(2 * M_WIDTH)),
        (jax.ShapeDtypeStruct((T // M_CHUNK, M_WIDTH, M_CHUNK), BF16),
         pl.BlockSpec((tm // M_CHUNK, M_WIDTH, M_CHUNK), lambda i: (i, 0, 0))),
        (jax.ShapeDtypeStruct((N_GATE_COLS, T), F32), pl.BlockSpec((N_GATE_COLS, tm), lambda i: (0, i))),
        (jax.ShapeDtypeStruct((T, M_WIDTH), BF16), row(M_WIDTH)),
        (jax.ShapeDtypeStruct((T, D_MODEL), BF16), row(D_MODEL)),
        (jax.ShapeDtypeStruct((T, D_MODEL), BF16), row(D_MODEL)),
    ]
    n_out_bf16 = ATT_WIDTH + 4 * KV_WIDTH + 4 * M_WIDTH + 2 * D_MODEL
    est = (D_MODEL * (n_qkv + n_m + n_t + n_sg) * 2 + 2 * tm * D_MODEL * 4
           + tm * (n_qkv + n_m + n_t + n_sg) * 6 + 2 * tm * n_out_bf16 * 2 + 8 * tm * V7X_LANES * 4)
    return pl.pallas_call(
        _proj_body,
        grid=(T // tm,),
        in_specs=[row(D_MODEL), _const_spec((1, D_MODEL)), _const_spec((D_MODEL, n_qkv)), _const_spec((D_MODEL, n_m)),
                  _const_spec((n_t, D_MODEL)), _const_spec((D_MODEL, n_sg)),
                  _const_spec((1, V7X_LANES)), _const_spec((1, V7X_LANES)), rope, rope,
                  _const_spec((N_GATE_COLS, tm))],
        out_specs=[s for _, s in outs],
        out_shape=[o for o, _ in outs],
        compiler_params=pltpu.CompilerParams(dimension_semantics=("parallel",), vmem_limit_bytes=_vmem_limit(est)),
        name="proj",
    )(x2d, norm_g, w_qkv, w_m, w_t, w_sg, q_gain, k_gain, cos_t, sin_t, gate_bias)


def _attn_body(q_ref, kd_ref, vd_ref, o_ref):
    tq = q_ref.shape[1]
    lane = lax.broadcasted_iota(jnp.int32, (tq, V7X_LANES), 1)
    lo = lane < ATT_HEAD_DIM
    pairs_per_kv = (ATT_HEADS // ATT_KV_HEADS) // 2
    n_key = kd_ref.shape[1]
    for g in range(ATT_KV_HEADS):
        for p in range(pairs_per_kv):
            c0 = (g * pairs_per_kv + p) * V7X_LANES
            q2 = q_ref[0, :, c0:c0 + V7X_LANES]
            outs = []
            for half, keep in enumerate((lo, jnp.logical_not(lo))):
                qm = jnp.where(keep, q2, jnp.zeros_like(q2))
                m_run, acc = None, None
                for k0 in range(0, n_key, ATT_KEY_BLOCK):
                    kd = kd_ref[0, k0:k0 + ATT_KEY_BLOCK, g * V7X_LANES:(g + 1) * V7X_LANES]
                    vd = vd_ref[0, k0:k0 + ATT_KEY_BLOCK, (2 * g + half) * V7X_LANES:(2 * g + half + 1) * V7X_LANES]
                    s = lax.dot_general(qm, kd, NT_DIMS, preferred_element_type=F32)
                    m_blk = jnp.max(s, axis=-1, keepdims=True)
                    m_new = m_blk if m_run is None else jnp.maximum(m_run, m_blk)
                    pv = jnp.dot(jnp.exp2(s - m_new).astype(BF16), vd, preferred_element_type=F32)
                    acc = pv if acc is None else jnp.exp2(m_run - m_new) * acc + pv
                    m_run = m_new
                d0 = (1 - half) * ATT_HEAD_DIM
                outs.append(acc * (1.0 / acc[:, d0:d0 + 1]))
            o_ref[0, :, c0:c0 + V7X_LANES] = jnp.where(lo, outs[0], outs[1]).astype(BF16)


def _attention(q, kd, vd):
    B, S, _ = q.shape
    tq = ATT_Q_TILE
    est = (2 * tq * ATT_WIDTH * 2 * 2 + 2 * S * 6 * KV_WIDTH * 2 + 3 * tq * S * 4)
    return pl.pallas_call(
        _attn_body,
        grid=(B, S // tq),
        in_specs=[pl.BlockSpec((1, tq, ATT_WIDTH), lambda b, i: (b, i, 0)),
                  pl.BlockSpec((1, S, 2 * KV_WIDTH), lambda b, i: (b, 0, 0)),
                  pl.BlockSpec((1, S, 4 * KV_WIDTH), lambda b, i: (b, 0, 0))],
        out_specs=pl.BlockSpec((1, tq, ATT_WIDTH), lambda b, i: (b, i, 0)),
        out_shape=jax.ShapeDtypeStruct((B, S, ATT_WIDTH), BF16),
        compiler_params=pltpu.CompilerParams(dimension_semantics=("parallel", "parallel"),
                                             vmem_limit_bytes=_vmem_limit(est)),
        name="attn",
    )(q, kd, vd)


def _split3(x):
    hi = x.astype(BF16)
    r = x - hi.astype(F32)
    mid = r.astype(BF16)
    lo = (r - mid.astype(F32)).astype(BF16)
    return hi, mid, lo


def _log_sigmoid(x):
    return jnp.minimum(x, 0.0) - jnp.log1p(jnp.exp(-jnp.abs(x)))


M_AUG = M_HEAD_DIM + 16


def _mlstm_body(q_ref, k_ref, vt_ref, g_ref, og_ref, hg_ref, o_ref,
                b_ref, x_ref, w_ref, bl_ref, gx_ref, ht_ref):
    L = M_CHUNK
    D = M_HEAD_DIM
    nc = vt_ref.shape[0]
    heads = q_ref.shape[2] // D
    head0 = pl.program_id(1) * heads
    streams = [(h, d) for h in range(heads) for d in range(2)]
    src = lax.broadcasted_iota(jnp.int32, (L, L), 0)
    dst = lax.broadcasted_iota(jnp.int32, (L, L), 1)
    visible = (src <= dst, src >= dst)
    tri = [jnp.where(v, 1.0, 0.0).astype(BF16) for v in visible]

    for s, (h, d) in enumerate(streams):
        li = g_ref[2 * d * M_HEADS + head0 + h]
        lf = _log_sigmoid(g_ref[(2 * d + 1) * M_HEADS + head0 + h])
        hi, mid, lo3 = _split3(lf)
        b = (jnp.dot(hi, tri[d], preferred_element_type=F32) + jnp.dot(mid, tri[d], preferred_element_type=F32)
             + jnp.dot(lo3, tri[d], preferred_element_type=F32))
        b_end = b[:, L - 1:L] if d == 0 else b[:, 0:1]
        g = b_end - b + li
        gmax = jnp.max(g, axis=1, keepdims=True)
        b_ref[s] = b
        x_ref[s] = li - b
        w_ref[s] = jnp.exp(g - gmax)
        bl_ref[s] = jnp.broadcast_to(b_end, (nc, L))
        gx_ref[s] = jnp.broadcast_to(gmax, (nc, L))

    ones_rows = jnp.where(lax.broadcasted_iota(jnp.int32, (M_AUG - D, L), 0) == 0, 1.0, 0.0).astype(BF16)

    def step(i, carry, *, complete):
        states, pending = carry
        for s in range(len(streams)):
            emit(i - 1, s, pending[s])
        if complete:
            write_out(i - 1)
            write_out(nc - i)
        nxt = [advance(i, s, states[s]) for s in range(len(streams))]
        return tuple(n[0] for n in nxt), tuple(n[1] for n in nxt)

    def chunk_of(i, d):
        return i if d == 0 else nc - 1 - i

    def value_rows(c, h):
        return jnp.concatenate([vt_ref[c, h * D:(h + 1) * D, :], ones_rows], axis=0)

    def advance(i, s, state):
        h, d = streams[s]
        ct, m = state
        c = chunk_of(i, d)
        r0 = c * L if isinstance(c, int) else pl.multiple_of(c * L, L)
        qc = q_ref[0, pl.ds(r0, L), h * D:(h + 1) * D]
        kc = k_ref[0, pl.ds(r0, L), h * D:(h + 1) * D]
        brow = b_ref[s, pl.ds(c, 1), :]
        xrow = x_ref[s, pl.ds(c, 1), :]
        wrow = w_ref[s, pl.ds(c, 1), :]
        b_end = bl_ref[s, pl.ds(c, 1), :]
        gmax = gx_ref[s, pl.ds(c, 1), :]

        ut = jnp.dot((value_rows(c, h).astype(F32) * wrow).astype(BF16), kc, preferred_element_type=F32)

        both = lax.dot_general(jnp.concatenate([kc, ct.astype(BF16)], axis=0), qc, NT_DIMS,
                               preferred_element_type=F32)
        xcol = jnp.broadcast_to(xrow, (L, L)).T
        dmat = jnp.where(visible[d], xcol + brow, -jnp.inf)
        a = jnp.max(dmat, axis=0, keepdims=True)
        pt = (both[:L] * jnp.exp(dmat - a)).astype(BF16)
        m_in = brow + m
        m_t = jnp.maximum(a, m_in)
        pend = (pt, both[L:], jnp.exp(a - m_t), jnp.exp(m_in - m_t), jnp.exp(-m_t))

        m_new = jnp.maximum(b_end + m, gmax)
        keep = jnp.exp(b_end + m - m_new)[:, :D]
        add = jnp.exp(gmax - m_new)[:, :D]
        return (keep * ct + add * ut, m_new), pend

    def emit(i, s, pend):
        h, d = streams[s]
        c = chunk_of(i, d)
        pt, inter, wi, wo, floor = pend
        intra = jnp.dot(value_rows(c, h), pt, preferred_element_type=F32)
        comb = wi * intra + wo * inter
        ht_ref[s, c] = comb[:D] * (1.0 / jnp.maximum(jnp.abs(comb[D:D + 1]), floor))

    def write_out(c):
        r0 = c * L if isinstance(c, int) else pl.multiple_of(c * L, L)
        for h in range(heads):
            ht = ht_ref[2 * h, c] + ht_ref[2 * h + 1, c]
            inv = lax.rsqrt(jnp.mean(ht * ht, axis=0, keepdims=True) + EPS)
            hn = (ht * inv * hg_ref[h * D:(h + 1) * D, 0:1]).T
            og = og_ref[0, pl.ds(r0, L), h * D:(h + 1) * D]
            o_ref[0, pl.ds(r0, L), h * D:(h + 1) * D] = (og.astype(F32) * hn).astype(BF16)

    assert nc % 2 == 0
    meet = nc // 2 + 1
    init = (jnp.zeros((M_AUG, D), F32), jnp.zeros((1, L), F32))
    first = [advance(0, s, init) for s in range(len(streams))]
    carry = (tuple(f[0] for f in first), tuple(f[1] for f in first))
    carry = lax.fori_loop(1, meet, functools.partial(step, complete=False), carry, unroll=M_STEP_UNROLL)
    _, pending = lax.fori_loop(meet, nc, functools.partial(step, complete=True), carry, unroll=M_STEP_UNROLL)
    for s in range(len(streams)):
        emit(nc - 1, s, pending[s])
    write_out(nc - 1)
    write_out(0)


def _mlstm(mqk, mvt, gates_t, og, head_gain):
    B, S, _ = mqk.shape
    L = M_CHUNK
    nc = S // L
    hb = M_HEADS_PER_STEP
    w = hb * M_HEAD_DIM
    groups = M_HEADS // hb

    def head_block(offset):
        return pl.BlockSpec((1, S, w), lambda b, h: (b, 0, offset + h))

    scratch = [pltpu.VMEM((2 * hb, nc, L), F32) for _ in range(5)] + [pltpu.VMEM((2 * hb, nc, M_HEAD_DIM, L), F32)]
    est = (2 * (2 * S * w * 2 + nc * w * L * 2 + N_GATE_COLS * nc * L * 4 + 2 * S * w * 2)
           + 2 * hb * nc * M_HEAD_DIM * L * 4 + 16 * M_AUG * L * 4 * 2 * hb)
    return pl.pallas_call(
        _mlstm_body,
        grid=(B, groups),
        in_specs=[head_block(0), head_block(groups),
                  pl.BlockSpec((nc, w, L), lambda b, h: (b, h, 0)),
                  pl.BlockSpec((N_GATE_COLS, nc, L), lambda b, h: (0, b, 0)),
                  head_block(0),
                  pl.BlockSpec((w, V7X_LANES), lambda b, h: (h, 0))],
        out_specs=head_block(0),
        out_shape=jax.ShapeDtypeStruct((B, S, M_WIDTH), BF16),
        scratch_shapes=scratch,
        compiler_params=pltpu.CompilerParams(dimension_semantics=("parallel", "arbitrary"),
                                             vmem_limit_bytes=_vmem_limit(est)),
        name="mlstm",
    )(mqk, mqk, mvt, gates_t, og, head_gain)


def _merge_ffn_body(x_ref, att_ref, mem_ref, ga_ref, gm_ref, wa_ref, wm_ref, wo_ref,
                    ng_ref, wg_ref, wu_ref, wd_ref, fin_ref, o_ref, *, apply_final_norm):
    ya = jnp.dot(att_ref[...], wa_ref[...], preferred_element_type=F32)
    ym = jnp.dot(mem_ref[...], wm_ref[...], preferred_element_type=F32)
    merged = (ga_ref[...].astype(F32) * ya + gm_ref[...].astype(F32) * ym).astype(BF16)
    x2 = x_ref[...] + jnp.dot(merged, wo_ref[...], preferred_element_type=F32)
    y = _swiglu_half_step(x2, ng_ref, wg_ref, wu_ref, wd_ref)
    if apply_final_norm:
        y = _rms_norm(y, fin_ref[...])
    o_ref[...] = y


def _merge_ffn(x2d, att, mem, ga, gm, w_ua, w_um, w_out, norm_g, w_gate, w_up, w_down, final_g, apply_final_norm):
    T = x2d.shape[0]
    tm = TOKEN_TILE

    def row(width):
        return pl.BlockSpec((tm, width), lambda i: (i, 0))

    est = (_ffn_vmem_bytes(tm) + (ATT_WIDTH + M_WIDTH + D_MODEL) * D_MODEL * 2
           + 2 * tm * (ATT_WIDTH + M_WIDTH + 2 * D_MODEL) * 2 + 4 * tm * D_MODEL * 4)
    return pl.pallas_call(
        functools.partial(_merge_ffn_body, apply_final_norm=apply_final_norm),
        grid=(T // tm,),
        in_specs=[row(D_MODEL), row(ATT_WIDTH), row(M_WIDTH), row(D_MODEL), row(D_MODEL),
                  _const_spec((ATT_WIDTH, D_MODEL)), _const_spec((M_WIDTH, D_MODEL)), _const_spec((D_MODEL, D_MODEL))]
        + _ffn_weight_specs() + [_const_spec((1, D_MODEL))],
        out_specs=row(D_MODEL),
        out_shape=jax.ShapeDtypeStruct((T, D_MODEL), F32),
        compiler_params=pltpu.CompilerParams(dimension_semantics=("parallel",), vmem_limit_bytes=_vmem_limit(est)),
        name="merge_ffn",
    )(x2d, att, mem, ga, gm, w_ua, w_um, w_out, norm_g, w_gate, w_up, w_down, final_g)


def _rope_tables(seq_len):
    t = np.arange(seq_len)
    inv_freq = ROPE_THETA ** (-np.arange(0, ROPE_AXIS_DIM, 2, dtype=np.float32) / ROPE_AXIS_DIM)
    ang = np.concatenate([(t // GRID_W).astype(np.float32)[:, None] * inv_freq,
                          (t % GRID_W).astype(np.float32)[:, None] * inv_freq], axis=-1).astype(np.float32)
    ang = jnp.asarray(ang)
    cos = jnp.repeat(jnp.cos(ang), 2, axis=-1)
    sin = jnp.repeat(jnp.sin(ang), 2, axis=-1) * jnp.tile(jnp.asarray([-1.0, 1.0], F32), ATT_HEAD_DIM // 2)
    return jnp.tile(cos, (1, 2)), jnp.tile(sin, (1, 2))


def _layer_params(l, ffn1_norm, ffn1_w_gate, ffn1_w_up, ffn1_w_down, mix_norm, w_in, q_norm, k_norm,
                  mlstm_gate_bias, mlstm_head_norm, w_up_att, w_up_mlstm, w_out,
                  ffn2_norm, ffn2_w_gate, ffn2_w_up, ffn2_w_down):
    offs = np.concatenate([[0], np.cumsum(IN_SPLITS)])

    def cols(first, last):
        return w_in[l][:, offs[first]:offs[last]].astype(BF16)

    return dict(
        ffn1=(ffn1_norm[l][None], ffn1_w_gate[l].astype(BF16), ffn1_w_up[l].astype(BF16), ffn1_w_down[l].astype(BF16)),
        ffn2=(ffn2_norm[l][None], ffn2_w_gate[l].astype(BF16), ffn2_w_up[l].astype(BF16), ffn2_w_down[l].astype(BF16)),
        mix_norm=mix_norm[l][None],
        w_qkv=cols(0, 3),
        w_m=cols(3, 5),
        w_t=cols(5, 7).T,
        w_sg=cols(7, 10),
        q_gain=jnp.tile(q_norm[l], 2)[None],
        k_gain=jnp.tile(k_norm[l], 2)[None],
        gate_bias=jnp.broadcast_to(mlstm_gate_bias[l][:, None], (N_GATE_COLS, TOKEN_TILE)),
        head_gain=jnp.broadcast_to(mlstm_head_norm[l][:, None], (M_WIDTH, V7X_LANES)),
        w_ua=w_up_att[l].astype(BF16), w_um=w_up_mlstm[l].astype(BF16), w_out=w_out[l].astype(BF16),
    )


def _trunk(x, layers, final_g):
    B, S, D = x.shape
    cos_t, sin_t = _rope_tables(S)
    x2d = x.reshape(B * S, D)
    for i, p in enumerate(layers):
        last = i == len(layers) - 1
        x1 = _ffn(x2d, *p["ffn1"])
        q, kd, vd, mqk, mvt, gates_t, og, ga, gm = _proj(x1, S, p["mix_norm"], p["w_qkv"], p["w_m"], p["w_t"], p["w_sg"],
                                                         p["q_gain"], p["k_gain"], cos_t, sin_t, p["gate_bias"])
        att = _attention(q.reshape(B, S, -1), kd.reshape(B, S, -1), vd.reshape(B, S, -1))
        mem = _mlstm(mqk.reshape(B, S, -1), mvt, gates_t.reshape(N_GATE_COLS, -1, M_CHUNK), og.reshape(B, S, -1),
                     p["head_gain"])
        x2d = _merge_ffn(x1, att.reshape(B * S, -1), mem.reshape(B * S, -1), ga, gm, p["w_ua"], p["w_um"], p["w_out"],
                         *p["ffn2"], final_g, last)
    return x2d.reshape(B, S, D)


def kernel(x_prompt, x_sample, ffn1_norm, ffn1_w_gate, ffn1_w_up, ffn1_w_down, mix_norm, w_in, q_norm, k_norm,
           mlstm_gate_bias, mlstm_head_norm, w_up_att, w_up_mlstm, w_out,
           ffn2_norm, ffn2_w_gate, ffn2_w_up, ffn2_w_down, final_norm):
    depth = w_in.shape[0]
    layers = [_layer_params(l, ffn1_norm, ffn1_w_gate, ffn1_w_up, ffn1_w_down, mix_norm, w_in, q_norm, k_norm,
                            mlstm_gate_bias, mlstm_head_norm, w_up_att, w_up_mlstm, w_out,
                            ffn2_norm, ffn2_w_gate, ffn2_w_up, ffn2_w_down) for l in range(depth)]
    final_g = final_norm[None]
    return _trunk(x_prompt, layers, final_g), _trunk(x_sample, layers, final_g)
```

```python
import functools

import jax
import jax.numpy as jnp
import numpy as np
from jax import lax
from jax.experimental import pallas as pl
from jax.experimental.pallas import tpu as pltpu

D_MODEL = 1024
D_FF = 2816
GRID_W = 64
EPS = 1e-6
ATT_HEADS = 8
ATT_KV_HEADS = 2
ATT_HEAD_DIM = 64
ROPE_AXIS_DIM = ATT_HEAD_DIM // 2
ROPE_THETA = 10000.0
M_HEADS = 4
M_HEAD_DIM = 128
ATT_WIDTH = ATT_HEADS * ATT_HEAD_DIM
KV_WIDTH = ATT_KV_HEADS * ATT_HEAD_DIM
M_WIDTH = M_HEADS * M_HEAD_DIM
N_GATE_COLS = 4 * M_HEADS
IN_SPLITS = (ATT_WIDTH, KV_WIDTH, KV_WIDTH, M_WIDTH, M_WIDTH, M_WIDTH, N_GATE_COLS, M_WIDTH, D_MODEL, D_MODEL)

V7X_LANES = 128
V7X_VMEM_BYTES = 64 * 1024 * 1024

TOKEN_TILE = 512
FFN_TILES_PER_STEP = 2
ATT_Q_TILE = 1024
ATT_KEY_BLOCK = 1024
M_CHUNK = 128
M_HEADS_PER_STEP = 4
M_STEP_UNROLL = 8

ATT_Q_SCALE = ATT_HEAD_DIM ** -0.5 * float(np.log2(np.e))

F32 = jnp.float32
BF16 = jnp.bfloat16
NT_DIMS = (((1,), (1,)), ((), ()))


def _vmem_limit(nbytes):
    return int(min(nbytes * 1.25 + (4 << 20), V7X_VMEM_BYTES - (6 << 20)))


def _rms_norm(x, g):
    return x * lax.rsqrt(jnp.mean(x * x, axis=-1, keepdims=True) + EPS) * g


def _const_spec(shape):
    return pl.BlockSpec(shape, lambda *_: (0,) * len(shape), pipeline_mode=pl.Buffered(1))


def _swiglu_half_step(x, ng_ref, wg_ref, wu_ref, wd_ref):
    xn = _rms_norm(x, ng_ref[...]).astype(BF16)
    g = jnp.dot(xn, wg_ref[...], preferred_element_type=F32)
    u = jnp.dot(xn, wu_ref[...], preferred_element_type=F32)
    a = (g * jax.nn.sigmoid(g) * u).astype(BF16)
    return x + 0.5 * jnp.dot(a, wd_ref[...], preferred_element_type=F32)


def _ffn_vmem_bytes(tm):
    return 3 * D_MODEL * D_FF * 2 + 4 * tm * D_MODEL * 4 + tm * D_FF * (4 + 4 + 2) + tm * D_MODEL * 8


def _ffn_weight_specs():
    return [_const_spec((1, D_MODEL)), _const_spec((D_MODEL, D_FF)), _const_spec((D_MODEL, D_FF)),
            _const_spec((D_FF, D_MODEL))]


def _ffn_body(x_hbm, ng_ref, wg_ref, wu_ref, wd_ref, o_hbm, *, rows):
    def tile(x_ref, o_ref):
        for r in range(0, rows, TOKEN_TILE):
            sl = slice(r, r + TOKEN_TILE)
            o_ref[sl, :] = _swiglu_half_step(x_ref[sl, :], ng_ref, wg_ref, wu_ref, wd_ref)

    row = pl.BlockSpec((rows, D_MODEL), lambda i: (i, 0))
    pltpu.emit_pipeline(tile, grid=(x_hbm.shape[0] // rows,), in_specs=[row], out_specs=[row])(x_hbm, o_hbm)


def _ffn(x2d, norm_g, w_gate, w_up, w_down):
    T = x2d.shape[0]
    tm = TOKEN_TILE
    rows = FFN_TILES_PER_STEP * tm
    whole_vmem = pl.BlockSpec(memory_space=pltpu.VMEM)
    est = _ffn_vmem_bytes(tm) + 4 * (rows - tm) * D_MODEL * 4 + tm * D_FF * (4 + 4 + 2)
    return pl.pallas_call(
        functools.partial(_ffn_body, rows=rows),
        in_specs=[pl.BlockSpec(memory_space=pl.ANY)] + [whole_vmem] * 4,
        out_specs=pl.BlockSpec(memory_space=pl.ANY),
        out_shape=jax.ShapeDtypeStruct((T, D_MODEL), F32),
        compiler_params=pltpu.CompilerParams(vmem_limit_bytes=_vmem_limit(est)),
        name="ffn",
    )(x2d, norm_g, w_gate, w_up, w_down)


def _norm_rope(x, gain, cos, sin_signed, lo, even):
    x2 = x * x
    ss_lo = jnp.sum(jnp.where(lo, x2, 0.0), axis=-1, keepdims=True)
    ss_hi = jnp.sum(jnp.where(lo, 0.0, x2), axis=-1, keepdims=True)
    inv = jnp.where(lo, lax.rsqrt(ss_lo / ATT_HEAD_DIM + EPS), lax.rsqrt(ss_hi / ATT_HEAD_DIM + EPS))
    y = x * inv * gain
    partner = jnp.where(even, pltpu.roll(y, V7X_LANES - 1, 1), pltpu.roll(y, 1, 1))
    return y * cos + partner * sin_signed


def _dup_halves(x, lo):
    xr = pltpu.roll(x, ATT_HEAD_DIM, 1)
    return jnp.where(lo, x, xr), jnp.where(lo, xr, x)


def _proj_body(x_ref, ng_ref, wqkv_ref, wm_ref, wt_ref, wsg_ref, qg_ref, kg_ref, cos_ref, sin_ref, bias_ref,
               q_ref, kd_ref, vd_ref, mqk_ref, mvt_ref, gt_ref, og_ref, ga_ref, gm_ref):
    h = _rms_norm(x_ref[...], ng_ref[...]).astype(BF16)
    tm = h.shape[0]
    lane = lax.broadcasted_iota(jnp.int32, (tm, V7X_LANES), 1)
    lo = lane < ATT_HEAD_DIM
    even = (lane & 1) == 0
    cos = cos_ref[...]
    sin = sin_ref[...]

    qkv = jnp.dot(h, wqkv_ref[...], preferred_element_type=F32)
    for j in range(ATT_WIDTH // V7X_LANES):
        sl = slice(j * V7X_LANES, (j + 1) * V7X_LANES)
        qj = _norm_rope(qkv[:, sl], qg_ref[...], cos, sin, lo, even)
        q_ref[:, sl] = (qj * ATT_Q_SCALE).astype(BF16)
    k = _norm_rope(qkv[:, ATT_WIDTH:ATT_WIDTH + KV_WIDTH], kg_ref[...], cos, sin, lo, even)
    k0, k1 = _dup_halves(k, lo)
    kd_ref[:, :V7X_LANES] = k0.astype(BF16)
    kd_ref[:, V7X_LANES:] = k1.astype(BF16)
    v = qkv[:, ATT_WIDTH + KV_WIDTH:]
    vr = pltpu.roll(v, ATT_HEAD_DIM, 1)
    one_lo = jnp.where(lane == 0, 1.0, 0.0)
    one_hi = jnp.where(lane == ATT_HEAD_DIM, 1.0, 0.0)
    for j, val in enumerate((jnp.where(lo, v, one_hi), jnp.where(lo, one_lo, vr),
                             jnp.where(lo, vr, one_hi), jnp.where(lo, one_lo, v))):
        vd_ref[:, j * V7X_LANES:(j + 1) * V7X_LANES] = val.astype(BF16)

    m = jnp.dot(h, wm_ref[...], preferred_element_type=F32)
    mqk_ref[:, :M_WIDTH] = m[:, :M_WIDTH].astype(BF16)
    mqk_ref[:, M_WIDTH:] = (m[:, M_WIDTH:] * (M_HEAD_DIM ** -0.5)).astype(BF16)

    t = lax.dot_general(wt_ref[...], h, NT_DIMS, preferred_element_type=F32)
    for j in range(tm // M_CHUNK):
        mvt_ref[j] = t[:M_WIDTH, j * M_CHUNK:(j + 1) * M_CHUNK].astype(BF16)
    gt_ref[...] = t[M_WIDTH:, :] + bias_ref[...]

    sg = jax.nn.sigmoid(jnp.dot(h, wsg_ref[...], preferred_element_type=F32))
    og_ref[...] = sg[:, :M_WIDTH].astype(BF16)
    ga_ref[...] = sg[:, M_WIDTH:M_WIDTH + D_MODEL].astype(BF16)
    gm_ref[...] = sg[:, M_WIDTH + D_MODEL:].astype(BF16)


def _proj(x2d, seq_len, norm_g, w_qkv, w_m, w_t, w_sg, q_gain, k_gain, cos_t, sin_t, gate_bias):
    T = x2d.shape[0]
    tm = TOKEN_TILE
    tiles_per_seq = seq_len // tm
    n_qkv, n_m, n_t, n_sg = w_qkv.shape[1], w_m.shape[1], w_t.shape[0], w_sg.shape[1]

    def row(width):
        return pl.BlockSpec((tm, width), lambda i: (i, 0))

    rope = pl.BlockSpec((tm, V7X_LANES), lambda i: (i % tiles_per_seq, 0))
    outs = [
        (jax.ShapeDtypeStruct((T, ATT_WIDTH), BF16), row(ATT_WIDTH)),
        (jax.ShapeDtypeStruct((T, 2 * KV_WIDTH), BF16), row(2 * KV_WIDTH)),
        (jax.ShapeDtypeStruct((T, 4 * KV_WIDTH), BF16), row(4 * KV_WIDTH)),
        (jax.ShapeDtypeStruct((T, 2 * M_WIDTH), BF16), row(2 * M_WIDTH)),
        (jax.ShapeDtypeStruct((T // M_CHUNK, M_WIDTH, M_CHUNK), BF16),
         pl.BlockSpec((tm // M_CHUNK, M_WIDTH, M_CHUNK), lambda i: (i, 0, 0))),
        (jax.ShapeDtypeStruct((N_GATE_COLS, T), F32), pl.BlockSpec((N_GATE_COLS, tm), lambda i: (0, i))),
        (jax.ShapeDtypeStruct((T, M_WIDTH), BF16), row(M_WIDTH)),
        (jax.ShapeDtypeStruct((T, D_MODEL), BF16), row(D_MODEL)),
        (jax.ShapeDtypeStruct((T, D_MODEL), BF16), row(D_MODEL)),
    ]
    n_out_bf16 = ATT_WIDTH + 4 * KV_WIDTH + 4 * M_WIDTH + 2 * D_MODEL
    est = (D_MODEL * (n_qkv + n_m + n_t + n_sg) * 2 + 2 * tm * D_MODEL * 4
           + tm * (n_qkv + n_m + n_t + n_sg) * 6 + 2 * tm * n_out_bf16 * 2 + 8 * tm * V7X_LANES * 4)
    return pl.pallas_call(
        _proj_body,
        grid=(T // tm,),
        in_specs=[row(D_MODEL), _const_spec((1, D_MODEL)), _const_spec((D_MODEL, n_qkv)), _const_spec((D_MODEL, n_m)),
                  _const_spec((n_t, D_MODEL)), _const_spec((D_MODEL, n_sg)),
                  _const_spec((1, V7X_LANES)), _const_spec((1, V7X_LANES)), rope, rope,
                  _const_spec((N_GATE_COLS, tm))],
        out_specs=[s for _, s in outs],
        out_shape=[o for o, _ in outs],
        compiler_params=pltpu.CompilerParams(dimension_semantics=("parallel",), vmem_limit_bytes=_vmem_limit(est)),
        name="proj",
    )(x2d, norm_g, w_qkv, w_m, w_t, w_sg, q_gain, k_gain, cos_t, sin_t, gate_bias)


def _attn_body(q_ref, kd_ref, vd_ref, o_ref):
    tq = q_ref.shape[1]
    lane = lax.broadcasted_iota(jnp.int32, (tq, V7X_LANES), 1)
    lo = lane < ATT_HEAD_DIM
    pairs_per_kv = (ATT_HEADS // ATT_KV_HEADS) // 2
    n_key = kd_ref.shape[1]
    for g in range(ATT_KV_HEADS):
        for p in range(pairs_per_kv):
            c0 = (g * pairs_per_kv + p) * V7X_LANES
            q2 = q_ref[0, :, c0:c0 + V7X_LANES]
            outs = []
            for half, keep in enumerate((lo, jnp.logical_not(lo))):
                qm = jnp.where(keep, q2, jnp.zeros_like(q2))
                m_run, acc = None, None
                for k0 in range(0, n_key, ATT_KEY_BLOCK):
                    kd = kd_ref[0, k0:k0 + ATT_KEY_BLOCK, g * V7X_LANES:(g + 1) * V7X_LANES]
                    vd = vd_ref[0, k0:k0 + ATT_KEY_BLOCK, (2 * g + half) * V7X_LANES:(2 * g + half + 1) * V7X_LANES]
                    s = lax.dot_general(qm, kd, NT_DIMS, preferred_element_type=F32)
                    m_blk = jnp.max(s, axis=-1, keepdims=True)
                    m_new = m_blk if m_run is None else jnp.maximum(m_run, m_blk)
                    pv = jnp.dot(jnp.exp2(s - m_new).astype(BF16), vd, preferred_element_type=F32)
                    acc = pv if acc is None else jnp.exp2(m_run - m_new) * acc + pv
                    m_run = m_new
                d0 = (1 - half) * ATT_HEAD_DIM
                outs.append(acc * (1.0 / acc[:, d0:d0 + 1]))
            o_ref[0, :, c0:c0 + V7X_LANES] = jnp.where(lo, outs[0], outs[1]).astype(BF16)


def _attention(q, kd, vd):
    B, S, _ = q.shape
    tq = ATT_Q_TILE
    est = (2 * tq * ATT_WIDTH * 2 * 2 + 2 * S * 6 * KV_WIDTH * 2 + 3 * tq * S * 4)
    return pl.pallas_call(
        _attn_body,
        grid=(B, S // tq),
        in_specs=[pl.BlockSpec((1, tq, ATT_WIDTH), lambda b, i: (b, i, 0)),
                  pl.BlockSpec((1, S, 2 * KV_WIDTH), lambda b, i: (b, 0, 0)),
                  pl.BlockSpec((1, S, 4 * KV_WIDTH), lambda b, i: (b, 0, 0))],
        out_specs=pl.BlockSpec((1, tq, ATT_WIDTH), lambda b, i: (b, i, 0)),
        out_shape=jax.ShapeDtypeStruct((B, S, ATT_WIDTH), BF16),
        compiler_params=pltpu.CompilerParams(dimension_semantics=("parallel", "parallel"),
                                             vmem_limit_bytes=_vmem_limit(est)),
        name="attn",
    )(q, kd, vd)


def _split3(x):
    hi = x.astype(BF16)
    r = x - hi.astype(F32)
    mid = r.astype(BF16)
    lo = (r - mid.astype(F32)).astype(BF16)
    return hi, mid, lo


def _log_sigmoid(x):
    return jnp.minimum(x, 0.0) - jnp.log1p(jnp.exp(-jnp.abs(x)))


M_AUG = M_HEAD_DIM + 16


def _mlstm_body(q_ref, k_ref, vt_ref, g_ref, og_ref, hg_ref, o_ref,
                b_ref, x_ref, w_ref, bl_ref, gx_ref, ht_ref):
    L = M_CHUNK
    D = M_HEAD_DIM
    nc = vt_ref.shape[0]
    heads = q_ref.shape[2] // D
    head0 = pl.program_id(1) * heads
    streams = [(h, d) for h in range(heads) for d in range(2)]
    src = lax.broadcasted_iota(jnp.int32, (L, L), 0)
    dst = lax.broadcasted_iota(jnp.int32, (L, L), 1)
    visible = (src <= dst, src >= dst)
    tri = [jnp.where(v, 1.0, 0.0).astype(BF16) for v in visible]

    for s, (h, d) in enumerate(streams):
        li = g_ref[2 * d * M_HEADS + head0 + h]
        lf = _log_sigmoid(g_ref[(2 * d + 1) * M_HEADS + head0 + h])
        hi, mid, lo3 = _split3(lf)
        b = (jnp.dot(hi, tri[d], preferred_element_type=F32) + jnp.dot(mid, tri[d], preferred_element_type=F32)
             + jnp.dot(lo3, tri[d], preferred_element_type=F32))
        b_end = b[:, L - 1:L] if d == 0 else b[:, 0:1]
        g = b_end - b + li
        gmax = jnp.max(g, axis=1, keepdims=True)
        b_ref[s] = b
        x_ref[s] = li - b
        w_ref[s] = jnp.exp(g - gmax)
        bl_ref[s] = jnp.broadcast_to(b_end, (nc, L))
        gx_ref[s] = jnp.broadcast_to(gmax, (nc, L))

    ones_rows = jnp.where(lax.broadcasted_iota(jnp.int32, (M_AUG - D, L), 0) == 0, 1.0, 0.0).astype(BF16)

    def step(i, carry, *, complete):
        states, pending = carry
        for s in range(len(streams)):
            emit(i - 1, s, pending[s])
        if complete:
            write_out(i - 1)
            write_out(nc - i)
        nxt = [advance(i, s, states[s]) for s in range(len(streams))]
        return tuple(n[0] for n in nxt), tuple(n[1] for n in nxt)

    def chunk_of(i, d):
        return i if d == 0 else nc - 1 - i

    def value_rows(c, h):
        return jnp.concatenate([vt_ref[c, h * D:(h + 1) * D, :], ones_rows], axis=0)

    def advance(i, s, state):
        h, d = streams[s]
        ct, m = state
        c = chunk_of(i, d)
        r0 = c * L if isinstance(c, int) else pl.multiple_of(c * L, L)
        qc = q_ref[0, pl.ds(r0, L), h * D:(h + 1) * D]
        kc = k_ref[0, pl.ds(r0, L), h * D:(h + 1) * D]
        brow = b_ref[s, pl.ds(c, 1), :]
        xrow = x_ref[s, pl.ds(c, 1), :]
        wrow = w_ref[s, pl.ds(c, 1), :]
        b_end = bl_ref[s, pl.ds(c, 1), :]
        gmax = gx_ref[s, pl.ds(c, 1), :]

        ut = jnp.dot((value_rows(c, h).astype(F32) * wrow).astype(BF16), kc, preferred_element_type=F32)

        both = lax.dot_general(jnp.concatenate([kc, ct.astype(BF16)], axis=0), qc, NT_DIMS,
                               preferred_element_type=F32)
        xcol = jnp.broadcast_to(xrow, (L, L)).T
        dmat = jnp.where(visible[d], xcol + brow, -jnp.inf)
        a = jnp.max(dmat, axis=0, keepdims=True)
        pt = (both[:L] * jnp.exp(dmat - a)).astype(BF16)
        m_in = brow + m
        m_t = jnp.maximum(a, m_in)
        pend = (pt, both[L:], jnp.exp(a - m_t), jnp.exp(m_in - m_t), jnp.exp(-m_t))

        m_new = jnp.maximum(b_end + m, gmax)
        keep = jnp.exp(b_end + m - m_new)[:, :D]
        add = jnp.exp(gmax - m_new)[:, :D]
        return (keep * ct + add * ut, m_new), pend

    def emit(i, s, pend):
        h, d = streams[s]
        c = chunk_of(i, d)
        pt, inter, wi, wo, floor = pend
        intra = jnp.dot(value_rows(c, h), pt, preferred_element_type=F32)
        comb = wi * intra + wo * inter
        ht_ref[s, c] = comb[:D] * (1.0 / jnp.maximum(jnp.abs(comb[D:D + 1]), floor))

    def write_out(c):
        r0 = c * L if isinstance(c, int) else pl.multiple_of(c * L, L)
        for h in range(heads):
            ht = ht_ref[2 * h, c] + ht_ref[2 * h + 1, c]
            inv = lax.rsqrt(jnp.mean(ht * ht, axis=0, keepdims=True) + EPS)
            hn = (ht * inv * hg_ref[h * D:(h + 1) * D, 0:1]).T
            og = og_ref[0, pl.ds(r0, L), h * D:(h + 1) * D]
            o_ref[0, pl.ds(r0, L), h * D:(h + 1) * D] = (og.astype(F32) * hn).astype(BF16)

    assert nc % 2 == 0
    meet = nc // 2 + 1
    init = (jnp.zeros((M_AUG, D), F32), jnp.zeros((1, L), F32))
    first = [advance(0, s, init) for s in range(len(streams))]
    carry = (tuple(f[0] for f in first), tuple(f[1] for f in first))
    carry = lax.fori_loop(1, meet, functools.partial(step, complete=False), carry, unroll=M_STEP_UNROLL)
    _, pending = lax.fori_loop(meet, nc, functools.partial(step, complete=True), carry, unroll=M_STEP_UNROLL)
    for s in range(len(streams)):
        emit(nc - 1, s, pending[s])
    write_out(nc - 1)
    write_out(0)


def _mlstm(mqk, mvt, gates_t, og, head_gain):
    B, S, _ = mqk.shape
    L = M_CHUNK
    nc = S // L
    hb = M_HEADS_PER_STEP
    w = hb * M_HEAD_DIM
    groups = M_HEADS // hb

    def head_block(offset):
        return pl.BlockSpec((1, S, w), lambda b, h: (b, 0, offset + h))

    scratch = [pltpu.VMEM((2 * hb, nc, L), F32) for _ in range(5)] + [pltpu.VMEM((2 * hb, nc, M_HEAD_DIM, L), F32)]
    est = (2 * (2 * S * w * 2 + nc * w * L * 2 + N_GATE_COLS * nc * L * 4 + 2 * S * w * 2)
           + 2 * hb * nc * M_HEAD_DIM * L * 4 + 16 * M_AUG * L * 4 * 2 * hb)
    return pl.pallas_call(
        _mlstm_body,
        grid=(B, groups),
        in_specs=[head_block(0), head_block(groups),
                  pl.BlockSpec((nc, w, L), lambda b, h: (b, h, 0)),
                  pl.BlockSpec((N_GATE_COLS, nc, L), lambda b, h: (0, b, 0)),
                  head_block(0),
                  pl.BlockSpec((w, V7X_LANES), lambda b, h: (h, 0))],
        out_specs=head_block(0),
        out_shape=jax.ShapeDtypeStruct((B, S, M_WIDTH), BF16),
        scratch_shapes=scratch,
        compiler_params=pltpu.CompilerParams(dimension_semantics=("parallel", "arbitrary"),
                                             vmem_limit_bytes=_vmem_limit(est)),
        name="mlstm",
    )(mqk, mqk, mvt, gates_t, og, head_gain)


def _merge_ffn_body(x_ref, att_ref, mem_ref, ga_ref, gm_ref, wa_ref, wm_ref, wo_ref,
                    ng_ref, wg_ref, wu_ref, wd_ref, fin_ref, o_ref, *, apply_final_norm):
    ya = jnp.dot(att_ref[...], wa_ref[...], preferred_element_type=F32)
    ym = jnp.dot(mem_ref[...], wm_ref[...], preferred_element_type=F32)
    merged = (ga_ref[...].astype(F32) * ya + gm_ref[...].astype(F32) * ym).astype(BF16)
    x2 = x_ref[...] + jnp.dot(merged, wo_ref[...], preferred_element_type=F32)
    y = _swiglu_half_step(x2, ng_ref, wg_ref, wu_ref, wd_ref)
    if apply_final_norm:
        y = _rms_norm(y, fin_ref[...])
    o_ref[...] = y


def _merge_ffn(x2d, att, mem, ga, gm, w_ua, w_um, w_out, norm_g, w_gate, w_up, w_down, final_g, apply_final_norm):
    T = x2d.shape[0]
    tm = TOKEN_TILE

    def row(width):
        return pl.BlockSpec((tm, width), lambda i: (i, 0))

    est = (_ffn_vmem_bytes(tm) + (ATT_WIDTH + M_WIDTH + D_MODEL) * D_MODEL * 2
           + 2 * tm * (ATT_WIDTH + M_WIDTH + 2 * D_MODEL) * 2 + 4 * tm * D_MODEL * 4)
    return pl.pallas_call(
        functools.partial(_merge_ffn_body, apply_final_norm=apply_final_norm),
        grid=(T // tm,),
        in_specs=[row(D_MODEL), row(ATT_WIDTH), row(M_WIDTH), row(D_MODEL), row(D_MODEL),
                  _const_spec((ATT_WIDTH, D_MODEL)), _const_spec((M_WIDTH, D_MODEL)), _const_spec((D_MODEL, D_MODEL))]
        + _ffn_weight_specs() + [_const_spec((1, D_MODEL))],
        out_specs=row(D_MODEL),
        out_shape=jax.ShapeDtypeStruct((T, D_MODEL), F32),
        compiler_params=pltpu.CompilerParams(dimension_semantics=("parallel",), vmem_limit_bytes=_vmem_limit(est)),
        name="merge_ffn",
    )(x2d, att, mem, ga, gm, w_ua, w_um, w_out, norm_g, w_gate, w_up, w_down, final_g)


def _rope_tables(seq_len):
    t = np.arange(seq_len)
    inv_freq = ROPE_THETA ** (-np.arange(0, ROPE_AXIS_DIM, 2, dtype=np.float32) / ROPE_AXIS_DIM)
    ang = np.concatenate([(t // GRID_W).astype(np.float32)[:, None] * inv_freq,
                          (t % GRID_W).astype(np.float32)[:, None] * inv_freq], axis=-1).astype(np.float32)
    ang = jnp.asarray(ang)
    cos = jnp.repeat(jnp.cos(ang), 2, axis=-1)
    sin = jnp.repeat(jnp.sin(ang), 2, axis=-1) * jnp.tile(jnp.asarray([-1.0, 1.0], F32), ATT_HEAD_DIM // 2)
    return jnp.tile(cos, (1, 2)), jnp.tile(sin, (1, 2))


def _layer_params(l, ffn1_norm, ffn1_w_gate, ffn1_w_up, ffn1_w_down, mix_norm, w_in, q_norm, k_norm,
                  mlstm_gate_bias, mlstm_head_norm, w_up_att, w_up_mlstm, w_out,
                  ffn2_norm, ffn2_w_gate, ffn2_w_up, ffn2_w_down):
    offs = np.concatenate([[0], np.cumsum(IN_SPLITS)])

    def cols(first, last):
        return w_in[l][:, offs[first]:offs[last]].astype(BF16)

    return dict(
        ffn1=(ffn1_norm[l][None], ffn1_w_gate[l].astype(BF16), ffn1_w_up[l].astype(BF16), ffn1_w_down[l].astype(BF16)),
        ffn2=(ffn2_norm[l][None], ffn2_w_gate[l].astype(BF16), ffn2_w_up[l].astype(BF16), ffn2_w_down[l].astype(BF16)),
        mix_norm=mix_norm[l][None],
        w_qkv=cols(0, 3),
        w_m=cols(3, 5),
        w_t=cols(5, 7).T,
        w_sg=cols(7, 10),
        q_gain=jnp.tile(q_norm[l], 2)[None],
        k_gain=jnp.tile(k_norm[l], 2)[None],
        gate_bias=jnp.broadcast_to(mlstm_gate_bias[l][:, None], (N_GATE_COLS, TOKEN_TILE)),
        head_gain=jnp.broadcast_to(mlstm_head_norm[l][:, None], (M_WIDTH, V7X_LANES)),
        w_ua=w_up_att[l].astype(BF16), w_um=w_up_mlstm[l].astype(BF16), w_out=w_out[l].astype(BF16),
    )


def _trunk(x, layers, final_g):
    B, S, D = x.shape
    cos_t, sin_t = _rope_tables(S)
    x2d = x.reshape(B * S, D)
    for i, p in enumerate(layers):
        last = i == len(layers) - 1
        x1 = _ffn(x2d, *p["ffn1"])
        q, kd, vd, mqk, mvt, gates_t, og, ga, gm = _proj(x1, S, p["mix_norm"], p["w_qkv"], p["w_m"], p["w_t"], p["w_sg"],
                                                         p["q_gain"], p["k_gain"], cos_t, sin_t, p["gate_bias"])
        att = _attention(q.reshape(B, S, -1), kd.reshape(B, S, -1), vd.reshape(B, S, -1))
        mem = _mlstm(mqk.reshape(B, S, -1), mvt, gates_t.reshape(N_GATE_COLS, -1, M_CHUNK), og.reshape(B, S, -1),
                     p["head_gain"])
        x2d = _merge_ffn(x1, att.reshape(B * S, -1), mem.reshape(B * S, -1), ga, gm, p["w_ua"], p["w_um"], p["w_out"],
                         *p["ffn2"], final_g, last)
    return x2d.reshape(B, S, D)


def kernel(x_prompt, x_sample, ffn1_norm, ffn1_w_gate, ffn1_w_up, ffn1_w_down, mix_norm, w_in, q_norm, k_norm,
           mlstm_gate_bias, mlstm_head_norm, w_up_att, w_up_mlstm, w_out,
           ffn2_norm, ffn2_w_gate, ffn2_w_up, ffn2_w_down, final_norm):
    depth = w_in.shape[0]
    layers = [_layer_params(l, ffn1_norm, ffn1_w_gate, ffn1_w_up, ffn1_w_down, mix_norm, w_in, q_norm, k_norm,
                            mlstm_gate_bias, mlstm_head_norm, w_up_att, w_up_mlstm, w_out,
                            ffn2_norm, ffn2_w_gate, ffn2_w_up, ffn2_w_down) for l in range(depth)]
    final_g = final_norm[None]
    return _trunk(x_prompt, layers, final_g), _trunk(x_sample, layers, final_g)
```
